```python
import math
import jax
import jax.numpy as jnp
from jax import lax
import numpy as np

D_MODEL = 1024
BATCH = 16
SEQ = 256
DEPTH = 4
DEC_BATCH = 2
DEC_SEQ = 2048
PAST_LEN = 256

GRID_W = 64
N_MIXERS = 2
N_ATTN_LAYERS = (DEPTH + 1) // 2
N_SSM_LAYERS = DEPTH // 2
N_MOD = 9
D_FF = 2816
MLA_HEADS = 16
Q_LORA = 512
KV_LORA = 256
QK_NOPE = 64
QK_ROPE = 32
V_HEAD = 64
QK_HEAD = QK_NOPE + QK_ROPE
MLA_IN = Q_LORA + KV_LORA + QK_ROPE
CACHE_DIM = KV_LORA + QK_ROPE
ROPE_BASE = 10000.0
Q_BLOCK = 128
SSM_EXPAND = 2
D_INNER = SSM_EXPAND * D_MODEL
SSM_HEADDIM = 64
SSM_HEADS = D_INNER // SSM_HEADDIM
SSM_GROUPS = 4
D_STATE = 128
CONV_W = 5
CONV_DIM = D_INNER + 2 * SSM_GROUPS * D_STATE
SSM_IN = D_INNER + CONV_DIM + 2 * SSM_HEADS
CHUNK = 128
EPS = 1e-6

kernel_name = 'hybrid_mla_ssd_prefix_diffusion_step'


def rmsnorm(x, g):
    x32 = x.astype(jnp.float32)
    y = x32 * lax.rsqrt(jnp.mean(x32 * x32, axis=-1, keepdims=True) + EPS)
    return (y * g.astype(jnp.float32)).astype(x.dtype)


def adaln_input(x, g, shift, scale):
    return rmsnorm(x, g) * (1 + scale) + shift


def modulation(sc, w, b):
    m = (sc @ w + b).reshape(sc.shape[0], N_MOD, D_MODEL)
    return [m[:, k, None, :] for k in range(N_MOD)]


def swiglu(h, w_in, w_out):
    gate, up = jnp.split(h @ w_in, 2, axis=-1)
    return (jax.nn.silu(gate) * up) @ w_out


def ffn_half(x, shift, scale, gate, g, w_in, w_out):
    return x + 0.5 * gate * swiglu(adaln_input(x, g, shift, scale), w_in, w_out)


def axial_rope_tables(seq_len):
    rows = seq_len // GRID_W
    row = jnp.repeat(jnp.arange(rows), GRID_W)
    col = jnp.tile(jnp.arange(GRID_W), rows)
    pos = jnp.stack([row, col], axis=-1).astype(jnp.float32)
    half = QK_ROPE // 2
    freqs = 1.0 / (ROPE_BASE ** (jnp.arange(0, half, 2, dtype=jnp.float32) / half))
    ang = pos[:, :, None] * freqs
    ang = jnp.broadcast_to(ang[:, :, None, :], (seq_len, 2, 2, half // 2))
    return jnp.cos(ang), jnp.sin(ang)


def apply_axial_rope(x, cos, sin):
    xr = x.astype(jnp.float32).reshape(x.shape[:-1] + (2, 2, QK_ROPE // 4))
    rot = jnp.stack([-xr[..., 1, :], xr[..., 0, :]], axis=-2)
    return (xr * cos + rot * sin).reshape(x.shape).astype(x.dtype)


def block_attention(q, k, v):
    b, lq, h, dk = q.shape
    nb = lq // Q_BLOCK
    qb = q.reshape(b, nb, Q_BLOCK, h, dk).transpose(1, 0, 2, 3, 4)
    k32 = k.astype(jnp.float32)
    v32 = v.astype(jnp.float32)
    scale = 1.0 / math.sqrt(dk)

    def one_block(qblk):
        s = jnp.einsum('bqhd,bkhd->bhqk', qblk.astype(jnp.float32), k32) * scale
        p = jax.nn.softmax(s, axis=-1)
        return jnp.einsum('bhqk,bkhd->bqhd', p, v32)

    o = lax.map(one_block, qb)
    return o.transpose(1, 0, 2, 3, 4).reshape(b, lq, h, -1).astype(q.dtype)


def mla_project(h, w_in, q_norm, kv_norm, wq_b):
    b, l, _ = h.shape
    q_a, kv_a, k_r = jnp.split(h @ w_in, [Q_LORA, Q_LORA + KV_LORA], axis=-1)
    q = (rmsnorm(q_a, q_norm) @ wq_b).reshape(b, l, MLA_HEADS, QK_HEAD)
    ckv = rmsnorm(kv_a, kv_norm)
    return q, ckv, k_r


def mla_expand_kv(ckv, k_r, wkv_b):
    b, l, _ = ckv.shape
    kv = (ckv @ wkv_b).reshape(b, l, MLA_HEADS, QK_NOPE + V_HEAD)
    k_nope, v = jnp.split(kv, [QK_NOPE], axis=-1)
    k = jnp.concatenate([k_nope, jnp.broadcast_to(k_r[:, :, None, :], (b, l, MLA_HEADS, QK_ROPE))], axis=-1)
    return k, v


def mla_context(h, w_in, q_norm, kv_norm, wq_b, wkv_b, wo):
    b, l, _ = h.shape
    q, ckv, k_r = mla_project(h, w_in, q_norm, kv_norm, wq_b)
    k, v = mla_expand_kv(ckv, k_r, wkv_b)
    o = block_attention(q, k, v)
    out = o.reshape(b, l, MLA_HEADS * V_HEAD) @ wo
    return out, jnp.concatenate([ckv, k_r], axis=-1)


def mla_latent(h, cache, cos, sin, w_in, q_norm, kv_norm, wq_b, wkv_b, wo):
    b, l, _ = h.shape
    q, ckv, k_r = mla_project(h, w_in, q_norm, kv_norm, wq_b)
    q = jnp.concatenate([q[..., :QK_NOPE], apply_axial_rope(q[..., QK_NOPE:], cos[:, None], sin[:, None])], axis=-1)
    k_r = apply_axial_rope(k_r, cos, sin)
    k_lat, v_lat = mla_expand_kv(ckv, k_r, wkv_b)
    k_ctx, v_ctx = mla_expand_kv(cache[..., :KV_LORA], cache[..., KV_LORA:], wkv_b)
    k = jnp.concatenate([k_ctx, k_lat], axis=1)
    v = jnp.concatenate([v_ctx, v_lat], axis=1)
    o = block_attention(q, k, v)
    return o.reshape(b, l, MLA_HEADS * V_HEAD) @ wo


def centred_depthwise_conv(x, w, bias):
    out = lax.conv_general_dilated(
        x, w[:, None, :], window_strides=(1,), padding=[(CONV_W // 2, CONV_W // 2)],
        dimension_numbers=('NWC', 'WIO', 'NWC'), feature_group_count=x.shape[-1])
    return out + bias


def ssd_chunked(x, dt, a, bmat, cmat, h0):
    b, l, h, p = x.shape
    g = SSM_GROUPS
    r = h // g
    nc = l // CHUNK
    xg = x.reshape(b, nc, CHUNK, g, r, p)
    dtg = dt.reshape(b, nc, CHUNK, g, r)
    bg = bmat.reshape(b, nc, CHUNK, g, D_STATE)
    cg = cmat.reshape(b, nc, CHUNK, g, D_STATE)
    cum = jnp.cumsum(dtg * a.reshape(g, r), axis=2)
    xdt = xg * dtg[..., None]
    seg = cum[:, :, :, None] - cum[:, :, None, :]
    mask = jnp.tril(jnp.ones((CHUNK, CHUNK), dtype=bool))[:, :, None, None]
    lmat = jnp.exp(jnp.where(mask, seg, -jnp.inf))
    cb = jnp.einsum('bcign,bcjgn->bcijg', cg, bg)
    y_diag = jnp.einsum('bcijg,bcijgr,bcjgrp->bcigrp', cb, lmat, xdt)
    decay_end = jnp.exp(cum[:, :, -1:] - cum)
    states = jnp.einsum('bcjgn,bcjgr,bcjgrp->bcgrpn', bg, decay_end, xdt)
    chunk_decay = jnp.exp(cum[:, :, -1])

    def step(hc, inp):
        st, dc = inp
        return hc * dc[..., None, None] + st, hc

    h_fin, h_in = lax.scan(step, h0.reshape(b, g, r, p, D_STATE),
                           (states.transpose(1, 0, 2, 3, 4, 5), chunk_decay.transpose(1, 0, 2, 3)))
    h_in = h_in.transpose(1, 0, 2, 3, 4, 5)
    y_off = jnp.einsum('bcign,bcigr,bcgrpn->bcigrp', cg, jnp.exp(cum), h_in)
    y = (y_diag + y_off).reshape(b, l, h, p)
    return y, h_fin.reshape(b, h, p, D_STATE)


def ssm_mixer(h, h0, w_in, conv_w, conv_b, dt_bias, a_log, d_skip, norm_g, w_out):
    b, l, _ = h.shape
    z, xbc, dt_raw = jnp.split(h @ w_in, [D_INNER, D_INNER + CONV_DIM], axis=-1)
    xbc = jax.nn.silu(centred_depthwise_conv(xbc, conv_w, conv_b))
    xs, bm, cm = jnp.split(xbc, [D_INNER, D_INNER + SSM_GROUPS * D_STATE], axis=-1)
    xs = xs.reshape(b, l, SSM_HEADS, SSM_HEADDIM).astype(jnp.float32)
    bm = bm.reshape(b, l, SSM_GROUPS, D_STATE).astype(jnp.float32)
    cm = cm.reshape(b, l, SSM_GROUPS, D_STATE).astype(jnp.float32)
    dt = jax.nn.softplus(dt_raw.reshape(b, l, 2, SSM_HEADS).astype(jnp.float32) + dt_bias.astype(jnp.float32))
    a = -jnp.exp(a_log.astype(jnp.float32))
    h0f = h0.astype(jnp.float32)
    y_f, h_f = ssd_chunked(xs, dt[:, :, 0], a[0], bm, cm, h0f[:, 0])
    y_b, h_b = ssd_chunked(xs[:, ::-1], dt[:, ::-1, 1], a[1], bm[:, ::-1], cm[:, ::-1], h0f[:, 1])
    dsum = (d_skip[0] + d_skip[1]).astype(jnp.float32)
    y = y_f + y_b[:, ::-1] + dsum[:, None] * xs
    y = y.reshape(b, l, D_INNER).astype(h.dtype)
    y = rmsnorm(y * jax.nn.silu(z), norm_g)
    return y @ w_out, jnp.stack([h_f, h_b], axis=1).astype(h.dtype)


def setup_inputs(seed: int = 0) -> dict:
    key = jax.random.key(seed)
    ks = jax.random.split(key, 32)

    def nrm(k, shape, scale):
        return scale * jax.random.normal(k, shape, jnp.float32)

    na, ns = N_ATTN_LAYERS, N_SSM_LAYERS
    dt0 = jnp.exp(jax.random.uniform(ks[20], (ns, 2, SSM_HEADS), jnp.float32, math.log(1e-3), math.log(1e-1)))
    dt_bias = dt0 + jnp.log(-jnp.expm1(-dt0))
    a_log = jnp.log(jax.random.uniform(ks[21], (ns, 2, SSM_HEADS), jnp.float32, 1.0, 16.0))
    return {
        'x_prompt': nrm(ks[0], (BATCH, SEQ, D_MODEL), 1.0),
        'x_sample': nrm(ks[1], (DEC_BATCH, DEC_SEQ, D_MODEL), 1.0),
        'cache_mla': nrm(ks[2], (DEC_BATCH, na, PAST_LEN, CACHE_DIM), 1.0),
        'state_ssm': nrm(ks[3], (DEC_BATCH, ns, 2, SSM_HEADS, SSM_HEADDIM, D_STATE), 0.5),
        'c': nrm(ks[4], (DEC_BATCH, D_MODEL), 1.0),
        'c_ctx': nrm(ks[5], (D_MODEL,), 1.0),
        'mod_w': nrm(ks[6], (DEPTH, D_MODEL, N_MOD * D_MODEL), 0.5 * D_MODEL ** -0.5),
        'mod_b': nrm(ks[7], (DEPTH, N_MOD * D_MODEL), 0.02),
        'norm_g': 1.0 + nrm(ks[8], (DEPTH, 3, D_MODEL), 0.05),
        'ffn_w_in': nrm(ks[9], (DEPTH, 2, D_MODEL, 2 * D_FF), D_MODEL ** -0.5),
        'ffn_w_out': nrm(ks[10], (DEPTH, 2, D_FF, D_MODEL), D_FF ** -0.5),
        'mla_w_in': nrm(ks[11], (na, D_MODEL, MLA_IN), D_MODEL ** -0.5),
        'mla_q_norm': 1.0 + nrm(ks[12], (na, Q_LORA), 0.05),
        'mla_kv_norm': 1.0 + nrm(ks[13], (na, KV_LORA), 0.05),
        'mla_wq_b': nrm(ks[14], (na, Q_LORA, MLA_HEADS * QK_HEAD), Q_LORA ** -0.5),
        'mla_wkv_b': nrm(ks[15], (na, KV_LORA, MLA_HEADS * (QK_NOPE + V_HEAD)), KV_LORA ** -0.5),
        'mla_wo': nrm(ks[16], (na, MLA_HEADS * V_HEAD, D_MODEL), (MLA_HEADS * V_HEAD) ** -0.5),
        'ssm_w_in': nrm(ks[17], (ns, D_MODEL, SSM_IN), D_MODEL ** -0.5),
        'ssm_conv_w': nrm(ks[18], (ns, CONV_W, CONV_DIM), CONV_W ** -0.5),
        'ssm_conv_b': nrm(ks[19], (ns, CONV_DIM), 0.02),
        'ssm_dt_bias': dt_bias,
        'ssm_a_log': a_log,
        'ssm_d': 1.0 + nrm(ks[22], (ns, 2, SSM_HEADS), 0.1),
        'ssm_norm_g': 1.0 + nrm(ks[23], (ns, D_INNER), 0.05),
        'ssm_w_out': nrm(ks[24], (ns, D_INNER, D_MODEL), D_INNER ** -0.5),
        'final_norm_g': 1.0 + nrm(ks[25], (D_MODEL,), 0.05),
    }


def reference(x_prompt, x_sample, cache_mla, state_ssm, c, c_ctx, mod_w, mod_b, norm_g,
              ffn_w_in, ffn_w_out, mla_w_in, mla_q_norm, mla_kv_norm, mla_wq_b, mla_wkv_b, mla_wo,
              ssm_w_in, ssm_conv_w, ssm_conv_b, ssm_dt_bias, ssm_a_log, ssm_d, ssm_norm_g, ssm_w_out,
              final_norm_g):
    cos, sin = axial_rope_tables(x_sample.shape[1])
    xp, xs = x_prompt, x_sample
    sc_ctx = jax.nn.silu(c_ctx)[None, :]
    sc_lat = jax.nn.silu(c)
    new_mla, new_ssm = [], []
    for i in range(DEPTH):
        m_ctx = modulation(sc_ctx, mod_w[i], mod_b[i])
        m_lat = modulation(sc_lat, mod_w[i], mod_b[i])
        xp = ffn_half(xp, m_ctx[0], m_ctx[1], m_ctx[2], norm_g[i, 0], ffn_w_in[i, 0], ffn_w_out[i, 0])
        xs = ffn_half(xs, m_lat[0], m_lat[1], m_lat[2], norm_g[i, 0], ffn_w_in[i, 0], ffn_w_out[i, 0])
        hp = adaln_input(xp, norm_g[i, 1], m_ctx[3], m_ctx[4])
        hs = adaln_input(xs, norm_g[i, 1], m_lat[3], m_lat[4])
        j = i // N_MIXERS
        if i % N_MIXERS == 0:
            wts = (mla_w_in[j], mla_q_norm[j], mla_kv_norm[j], mla_wq_b[j], mla_wkv_b[j], mla_wo[j])
            op, ctx_entry = mla_context(hp, *wts)
            os_ = mla_latent(hs, cache_mla[:, j], cos, sin, *wts)
            new_mla.append(ctx_entry)
        else:
            wts = (ssm_w_in[j], ssm_conv_w[j], ssm_conv_b[j], ssm_dt_bias[j], ssm_a_log[j], ssm_d[j],
                   ssm_norm_g[j], ssm_w_out[j])
            zero_state = jnp.zeros((xp.shape[0], 2, SSM_HEADS, SSM_HEADDIM, D_STATE), xp.dtype)
            op, ctx_state = ssm_mixer(hp, zero_state, *wts)
            os_, _ = ssm_mixer(hs, state_ssm[:, j], *wts)
            new_ssm.append(ctx_state)
        xp = xp + m_ctx[5] * op
        xs = xs + m_lat[5] * os_
        xp = ffn_half(xp, m_ctx[6], m_ctx[7], m_ctx[8], norm_g[i, 2], ffn_w_in[i, 1], ffn_w_out[i, 1])
        xs = ffn_half(xs, m_lat[6], m_lat[7], m_lat[8], norm_g[i, 2], ffn_w_in[i, 1], ffn_w_out[i, 1])
    y_prompt = rmsnorm(xp, final_norm_g)
    y_sample = rmsnorm(xs, final_norm_g)
    new_cache_mla = jnp.stack(new_mla, axis=1)
    new_state_ssm = jnp.stack(new_ssm, axis=1)
    return (y_prompt, y_sample, new_cache_mla, new_state_ssm)
```

```python
import functools
import math

import jax
import jax.numpy as jnp
from jax import lax
from jax.experimental import pallas as pl
from jax.experimental.pallas import tpu as pltpu

F32 = jnp.float32
BF16 = jnp.bfloat16

D_MODEL = 1024
DEPTH = 4
N_MOD = 9
D_FF = 2816
GRID_W = 64
MLA_HEADS = 16
Q_LORA = 512
KV_LORA = 256
QK_NOPE = 64
QK_ROPE = 32
V_HEAD = 64
QK_HEAD = QK_NOPE + QK_ROPE
ROPE_BASE = 10000.0
HEAD_PAD = 128
D_INNER = 2048
SSM_HEADDIM = 64
SSM_HEADS = 32
SSM_GROUPS = 4
D_STATE = 128
CONV_W = 5
CONV_DIM = D_INNER + 2 * SSM_GROUPS * D_STATE
CHUNK = 128
EPS = 1e-6

LANES = 128
SUBLANES = 8
TOKEN_TILE = 512
FF_CHUNK = 256
VMEM_LIMIT = 56 * 1024 * 1024


def _cparams(sem):
    return pltpu.CompilerParams(dimension_semantics=sem, vmem_limit_bytes=VMEM_LIMIT)


def _const_spec(shape):
    nd = len(shape)
    return pl.BlockSpec(shape, lambda *_: (0,) * nd, pipeline_mode=pl.Buffered(1))


def _rmsnorm(x, g):
    ms = jnp.mean(x * x, axis=-1, keepdims=True)
    return (x * lax.rsqrt(ms + EPS)) * g


def _silu(x):
    return x * (1.0 / (1.0 + jnp.exp(-x)))


def _adaln(x, g, shift, scale):
    return _rmsnorm(x, g) * (1.0 + scale) + shift


class _Group:
    def __init__(self, row0, nb, length):
        self.row0, self.nb, self.length = row0, nb, length


def _mod_kernel(sc_ref, w_ref, b_ref, o_ref):
    s = _silu(sc_ref[...])
    o_ref[...] = jnp.dot(s.astype(BF16), w_ref[...].astype(BF16),
                         preferred_element_type=F32) + b_ref[...]


def _modulation(sc, mod_w, mod_b):
    tn = 1024
    n = N_MOD * D_MODEL
    return pl.pallas_call(
        _mod_kernel,
        grid=(DEPTH, n // tn),
        in_specs=[
            pl.BlockSpec((SUBLANES, D_MODEL), lambda i, j: (0, 0)),
            pl.BlockSpec((None, D_MODEL, tn), lambda i, j: (i, 0, j)),
            pl.BlockSpec((None, 1, tn), lambda i, j: (i, 0, j)),
        ],
        out_specs=pl.BlockSpec((None, SUBLANES, tn), lambda i, j: (i, 0, j)),
        out_shape=jax.ShapeDtypeStruct((DEPTH, SUBLANES, n), F32),
        compiler_params=_cparams(("arbitrary", "arbitrary")),
        name="modulation",
    )(sc, mod_w, mod_b.reshape(DEPTH, 1, n))


def _mod_spec(groups, tile, slot):
    bounds = []
    gid = 0
    for g in groups:
        for b in range(g.nb if g.row0 else 1):
            bounds.append((g.row0 + b * g.length) // tile)
            gid += 1

    def index_map(i, *_):
        gidx = 0
        for lo in bounds[1:]:
            gidx = gidx + (i >= lo).astype(jnp.int32)
        return (gidx, slot, 0, 0)

    return pl.BlockSpec((None, None, 3, D_MODEL), index_map)


def _ffn_kernel(x_ref, mod_ref, g_ref, win_ref, wout_ref, fg_ref, o_ref, a_scr, *, final_norm):
    x = x_ref[...]
    m = mod_ref[...]
    hb = _adaln(x, g_ref[...], m[0:1], m[1:2]).astype(BF16)
    for c in range(D_FF // FF_CHUNK):
        lo = c * FF_CHUNK
        gate = jnp.dot(hb, win_ref[:, lo:lo + FF_CHUNK], preferred_element_type=F32)
        up = jnp.dot(hb, win_ref[:, D_FF + lo:D_FF + lo + FF_CHUNK], preferred_element_type=F32)
        a_scr[:, lo:lo + FF_CHUNK] = (_silu(gate) * up).astype(BF16)
    y = jnp.dot(a_scr[...], wout_ref[...], preferred_element_type=F32)
    out = x + (0.5 * m[2:3]) * y
    if final_norm:
        out = _rmsnorm(out, fg_ref[...])
    o_ref[...] = out


def _ffn_half(x, modt, groups, slot, g, w_in, w_out, final_g, final_norm):
    t = x.shape[0]
    tm = TOKEN_TILE
    return pl.pallas_call(
        functools.partial(_ffn_kernel, final_norm=final_norm),
        grid=(t // tm,),
        in_specs=[
            pl.BlockSpec((tm, D_MODEL), lambda i: (i, 0)),
            _mod_spec(groups, tm, slot),
            _const_spec((1, D_MODEL)),
            _const_spec((D_MODEL, 2 * D_FF)),
            _const_spec((D_FF, D_MODEL)),
            _const_spec((1, D_MODEL)),
        ],
        out_specs=pl.BlockSpec((tm, D_MODEL), lambda i: (i, 0)),
        out_shape=jax.ShapeDtypeStruct((t, D_MODEL), F32),
        scratch_shapes=[pltpu.VMEM((tm, D_FF), BF16)],
        compiler_params=_cparams(("arbitrary",)),
        name="ffn_half",
    )(x, modt, g.reshape(1, D_MODEL), w_in, w_out, final_g.reshape(1, D_MODEL))


def _expand_kv(ckv_b, krope, wk_ref, wv_ref, k_ref, v_ref):
    kn = jnp.dot(ckv_b, wk_ref[...], preferred_element_type=F32)
    for h in range(MLA_HEADS):
        lo = h * HEAD_PAD
        k_ref[:, lo:lo + HEAD_PAD] = (kn[:, lo:lo + HEAD_PAD] + krope).astype(BF16)
    v_ref[...] = jnp.dot(ckv_b, wv_ref[...], preferred_element_type=F32).astype(BF16)


def _mla_proj_kernel(x_ref, mod_ref, g_ref, win_ref, qn_ref, kvn_ref, wq1_ref, wq2_ref,
                     wk_ref, wv_ref, cos_ref, sin_ref,
                     q_ref, k_ref, v_ref, ckv_ref, kr_ref):
    x = x_ref[...]
    m = mod_ref[...]
    hb = _adaln(x, g_ref[...], m[0:1], m[1:2]).astype(BF16)
    a = jnp.dot(hb, win_ref[...], preferred_element_type=F32)
    qa = _rmsnorm(a[:, :Q_LORA], qn_ref[...]).astype(BF16)
    ckv = _rmsnorm(a[:, Q_LORA:Q_LORA + KV_LORA], kvn_ref[...])
    kr = a[:, Q_LORA + KV_LORA:Q_LORA + KV_LORA + HEAD_PAD]
    kr_rot = a[:, Q_LORA + KV_LORA + HEAD_PAD:]
    cosk = cos_ref[...]
    sink = sin_ref[...]
    lane = lax.broadcasted_iota(jnp.int32, cosk.shape, 1)
    scale = 1.0 / math.sqrt(QK_HEAD)
    cosq = (cosk + (lane < QK_NOPE).astype(F32)) * scale
    sinq = sink * scale
    q1 = jnp.dot(qa, wq1_ref[...], preferred_element_type=F32)
    q2 = jnp.dot(qa, wq2_ref[...], preferred_element_type=F32)
    for h in range(MLA_HEADS):
        lo = h * HEAD_PAD
        q_ref[:, lo:lo + HEAD_PAD] = (q1[:, lo:lo + HEAD_PAD] * cosq
                                      + q2[:, lo:lo + HEAD_PAD] * sinq).astype(BF16)
    ckv_ref[...] = ckv
    kr_ref[...] = kr
    _expand_kv(ckv.astype(BF16), kr * cosk + kr_rot * sink, wk_ref, wv_ref, k_ref, v_ref)


def _mla_ctx_kernel(ckv_ref, kr_ref, wk_ref, wv_ref, k_ref, v_ref):
    _expand_kv(ckv_ref[...].astype(BF16), kr_ref[...], wk_ref, wv_ref, k_ref, v_ref)


def _mla_attn_kernel(*refs, n_parts, pairs):
    q_ref = refs[0]
    kv_refs = refs[1:1 + 2 * n_parts]
    o_ref = refs[1 + 2 * n_parts]
    tq = q_ref.shape[0]
    lane = lax.broadcasted_iota(jnp.int32, (tq, LANES), 1)
    for p in range(pairs):
        vs = [kv_refs[2 * i + 1][:, p * LANES:(p + 1) * LANES] for i in range(n_parts)]
        halves = []
        for h in (2 * p, 2 * p + 1):
            qh = q_ref[:, h * HEAD_PAD:(h + 1) * HEAD_PAD]
            ss = [lax.dot_general(qh, kv_refs[2 * i][:, h * HEAD_PAD:(h + 1) * HEAD_PAD],
                                  (((1,), (1,)), ((), ())), preferred_element_type=F32)
                  for i in range(n_parts)]
            mx = ss[0].max(axis=-1, keepdims=True)
            for s in ss[1:]:
                mx = jnp.maximum(mx, s.max(axis=-1, keepdims=True))
            den = None
            acc = None
            for s, v in zip(ss, vs):
                e = jnp.exp(s - mx)
                d = e.sum(axis=-1, keepdims=True)
                o = jnp.dot(e.astype(BF16), v, preferred_element_type=F32)
                den = d if den is None else den + d
                acc = o if acc is None else acc + o
            halves.append(acc / den)
        o_ref[:, p * LANES:(p + 1) * LANES] = jnp.where(lane < V_HEAD, halves[0], halves[1]).astype(BF16)


def _mla_attention(q, parts, grp, tq, pairs):
    nq = grp.length // tq
    qoff = grp.row0 // tq
    in_specs = [pl.BlockSpec((tq, 2 * pairs * HEAD_PAD), lambda b, p, i: (qoff + b * nq + i, p))]
    args = [q]
    for k, v, row0, lk in parts:
        koff = row0 // lk
        in_specs.append(pl.BlockSpec((lk, 2 * pairs * HEAD_PAD), lambda b, p, i, koff=koff: (koff + b, p)))
        in_specs.append(pl.BlockSpec((lk, 2 * pairs * V_HEAD), lambda b, p, i, koff=koff: (koff + b, p)))
        args += [k, v]
    return pl.pallas_call(
        functools.partial(_mla_attn_kernel, n_parts=len(parts), pairs=pairs),
        grid=(grp.nb, MLA_HEADS // (2 * pairs), nq),
        in_specs=in_specs,
        out_specs=pl.BlockSpec((tq, 2 * pairs * V_HEAD), lambda b, p, i: (b * nq + i, p)),
        out_shape=jax.ShapeDtypeStruct((grp.nb * grp.length, MLA_HEADS * V_HEAD), BF16),
        compiler_params=_cparams(("arbitrary", "arbitrary", "arbitrary")),
        name="mla_attention",
    )(*args)


def _two_group_specs(groups, tile, width):
    prompt, sample = groups
    n_p = prompt.nb * prompt.length // tile
    n_s = sample.nb * sample.length // tile
    return [pl.BlockSpec((tile, width), lambda i: (jnp.minimum(i, n_p - 1), 0)),
            pl.BlockSpec((tile, width), lambda i: (jnp.clip(i - n_p, 0, n_s - 1), 0))], n_p


def _mla_out_kernel(op_ref, os_ref, wo_ref, x_ref, mod_ref, o_ref, *, n_prompt_tiles):
    o = jnp.where(pl.program_id(0) < n_prompt_tiles, op_ref[...], os_ref[...])
    y = jnp.dot(o, wo_ref[...], preferred_element_type=F32)
    o_ref[...] = x_ref[...] + mod_ref[...][2:3] * y


def _rope_tables(groups, t):
    half = QK_ROPE // 2
    freqs = 1.0 / (ROPE_BASE ** (jnp.arange(0, half, 2, dtype=F32) / half))
    cos = jnp.zeros((t, HEAD_PAD), F32)
    sin = jnp.zeros((t, HEAD_PAD), F32)
    lo = QK_NOPE
    cos = cos.at[:, lo:lo + QK_ROPE].set(1.0)
    for g in groups:
        if not g.row0:
            continue
        pos_i = jnp.arange(g.length)
        pos = jnp.stack([pos_i // GRID_W, pos_i % GRID_W], axis=-1).astype(F32)
        ang = pos[:, :, None] * freqs
        ang = jnp.broadcast_to(ang[:, :, None, :], (g.length, 2, 2, half // 2)).reshape(g.length, QK_ROPE)
        ang = jnp.tile(ang, (g.nb, 1))
        rows = slice(g.row0, g.row0 + g.nb * g.length)
        cos = cos.at[rows, lo:lo + QK_ROPE].set(jnp.cos(ang))
        sin = sin.at[rows, lo:lo + QK_ROPE].set(jnp.sin(ang))
    return cos, sin


def _rot_cols(w):
    q = QK_ROPE // 4
    w4 = w.reshape(w.shape[:-1] + (2, 2, q))
    return jnp.stack([-w4[..., 1, :], w4[..., 0, :]], axis=-2).reshape(w.shape)


def _mla_weights(w_in, wq_b, wkv_b):
    d = w_in.shape[0]
    z = lambda *s: jnp.zeros(s, F32)
    k_r = w_in[:, Q_LORA + KV_LORA:]
    pad_tile = lambda c: jnp.concatenate([z(d, QK_NOPE), c, z(d, HEAD_PAD - QK_HEAD)], axis=1)
    w_in_ext = jnp.concatenate([w_in[:, :Q_LORA + KV_LORA], pad_tile(k_r), pad_tile(_rot_cols(k_r))], axis=1)
    wq = wq_b.reshape(Q_LORA, MLA_HEADS, QK_HEAD)
    zq = lambda n: z(Q_LORA, MLA_HEADS, n)
    wq1 = jnp.concatenate([wq, zq(HEAD_PAD - QK_HEAD)], axis=-1)
    wq2 = jnp.concatenate([zq(QK_NOPE), _rot_cols(wq[..., QK_NOPE:]), zq(HEAD_PAD - QK_HEAD)], axis=-1)
    wkv = wkv_b.reshape(KV_LORA, MLA_HEADS, QK_NOPE + V_HEAD)
    wk = jnp.concatenate([wkv[..., :QK_NOPE], z(KV_LORA, MLA_HEADS, HEAD_PAD - QK_NOPE)], axis=-1)
    wv = wkv[..., QK_NOPE:]
    flat = lambda w: w.reshape(w.shape[0], -1).astype(BF16)
    return w_in_ext.astype(BF16), flat(wq1), flat(wq2), flat(wk), flat(wv)


def _mla_layer(x, modt, groups, slot, g, cache, cos, sin, w_in, q_norm, kv_norm, wq_b, wkv_b, wo):
    t = x.shape[0]
    tm = TOKEN_TILE
    w_in_ext, wq1, wq2, wk, wv = _mla_weights(w_in, wq_b, wkv_b)
    hd = MLA_HEADS * HEAD_PAD
    vd = MLA_HEADS * V_HEAD
    row = lambda c: pl.BlockSpec((tm, c), lambda i: (i, 0))
    q, k, v, ckv, kr = pl.pallas_call(
        _mla_proj_kernel,
        grid=(t // tm,),
        in_specs=[
            row(D_MODEL), _mod_spec(groups, tm, slot), _const_spec((1, D_MODEL)),
            _const_spec(w_in_ext.shape), _const_spec((1, Q_LORA)), _const_spec((1, KV_LORA)),
            _const_spec(wq1.shape), _const_spec(wq2.shape), _const_spec(wk.shape), _const_spec(wv.shape),
            row(HEAD_PAD), row(HEAD_PAD),
        ],
        out_specs=[row(hd), row(hd), row(vd), row(KV_LORA), row(HEAD_PAD)],
        out_shape=[jax.ShapeDtypeStruct((t, hd), BF16), jax.ShapeDtypeStruct((t, hd), BF16),
                   jax.ShapeDtypeStruct((t, vd), BF16), jax.ShapeDtypeStruct((t, KV_LORA), F32),
                   jax.ShapeDtypeStruct((t, HEAD_PAD), F32)],
        compiler_params=_cparams(("arbitrary",)),
        name="mla_project",
    )(x, modt, g.reshape(1, D_MODEL), w_in_ext, q_norm.reshape(1, -1), kv_norm.reshape(1, -1),
      wq1, wq2, wk, wv, cos, sin)

    prompt, sample = groups
    nb, past, _ = cache.shape
    cflat = cache.reshape(nb * past, -1)
    ckv_c = cflat[:, :KV_LORA]
    kr_c = jnp.pad(cflat[:, KV_LORA:], ((0, 0), (QK_NOPE, HEAD_PAD - QK_HEAD)))
    crow = lambda c: pl.BlockSpec((past, c), lambda i: (i, 0))
    k_c, v_c = pl.pallas_call(
        _mla_ctx_kernel,
        grid=(nb,),
        in_specs=[crow(KV_LORA), crow(HEAD_PAD), _const_spec(wk.shape), _const_spec(wv.shape)],
        out_specs=[crow(hd), crow(vd)],
        out_shape=[jax.ShapeDtypeStruct((nb * past, hd), BF16), jax.ShapeDtypeStruct((nb * past, vd), BF16)],
        compiler_params=_cparams(("arbitrary",)),
        name="mla_ctx_expand",
    )(ckv_c, kr_c, wk, wv)

    o_p = _mla_attention(q, [(k, v, prompt.row0, prompt.length)], prompt, 256, MLA_HEADS // 2)
    o_s = _mla_attention(q, [(k_c, v_c, 0, past), (k, v, sample.row0, sample.length)], sample, 512, 1)
    o_specs, n_p = _two_group_specs(groups, tm, vd)
    x = pl.pallas_call(
        functools.partial(_mla_out_kernel, n_prompt_tiles=n_p),
        grid=(t // tm,),
        in_specs=o_specs + [_const_spec((vd, D_MODEL)), row(D_MODEL), _mod_spec(groups, tm, slot)],
        out_specs=row(D_MODEL),
        out_shape=jax.ShapeDtypeStruct((t, D_MODEL), F32),
        compiler_params=_cparams(("arbitrary",)),
        name="mla_out",
    )(o_p, o_s, wo.astype(BF16), x, modt)
    np_rows = prompt.nb * prompt.length
    entry = jnp.concatenate([ckv[:np_rows], kr[:np_rows, QK_NOPE:QK_HEAD]], axis=-1)
    return x, entry.reshape(prompt.nb, prompt.length, -1)


def _ssm_proj_kernel(x_ref, mod_ref, g_ref, wz_ref, wx_ref, wdt_ref, z_ref, xbc_ref, dt_ref):
    m = mod_ref[...]
    hb = _adaln(x_ref[...], g_ref[...], m[0:1], m[1:2]).astype(BF16)
    z_ref[...] = jnp.dot(hb, wz_ref[...], preferred_element_type=F32)
    xbc_ref[...] = jnp.dot(hb, wx_ref[...], preferred_element_type=F32)
    dt_ref[...] = jnp.dot(hb, wdt_ref[...], preferred_element_type=F32)


def _split3(a):
    hi = a.astype(BF16)
    r1 = a - hi.astype(F32)
    mid = r1.astype(BF16)
    lo = (r1 - mid.astype(F32)).astype(BF16)
    return hi, mid, lo


def _ssd_kernel(*refs, reverse, zero_init, emit_state, nc):
    it = iter(refs)
    cur_ref, prev_ref, next_ref, dt_ref = next(it), next(it), next(it), next(it)
    cw_ref, cb_ref, dtb_ref, alog_ref, dskip_ref = next(it), next(it), next(it), next(it), next(it)
    h0_ref = None if zero_init else next(it)
    y_ref = next(it)
    st_ref = next(it) if emit_state else None
    state, xpad, act = next(it), next(it), next(it)

    q = CHUNK
    c = pl.program_id(1)
    ce = (nc - 1 - c) if reverse else c

    @pl.when(c == 0)
    def _():
        if zero_init:
            state[...] = jnp.zeros(state.shape, F32)
        else:
            state[...] = h0_ref[...]

    xpad[0:SUBLANES, :] = jnp.where(ce > 0, prev_ref[...], 0.0)
    xpad[SUBLANES:SUBLANES + q, :] = cur_ref[...]
    xpad[SUBLANES + q:, :] = jnp.where(ce < nc - 1, next_ref[...], 0.0)
    cw = 512
    for j in range(CONV_DIM // cw):
        cols = slice(j * cw, (j + 1) * cw)
        acc = jnp.broadcast_to(cb_ref[:, cols], (q, cw))
        for k in range(CONV_W):
            acc = acc + xpad[SUBLANES - CONV_W // 2 + k:SUBLANES - CONV_W // 2 + k + q, cols] * cw_ref[k:k + 1, cols]
        act[:, cols] = _silu(acc)

    dtr = dt_ref[...] + dtb_ref[...]
    dtv = jnp.maximum(dtr, 0.0) + jnp.log1p(jnp.exp(-jnp.abs(dtr)))
    a = dtv * (-jnp.exp(alog_ref[...]))
    ri = lax.broadcasted_iota(jnp.int32, (q, q), 0)
    ci = lax.broadcasted_iota(jnp.int32, (q, q), 1)
    keep = (ci >= ri) if reverse else (ci <= ri)
    tri = keep.astype(BF16)
    cum = sum(jnp.dot(tri, part, preferred_element_type=F32) for part in _split3(a))
    tot = jnp.sum(a, axis=0, keepdims=True)
    ecum = jnp.exp(cum)
    etot = jnp.exp(tot)
    cum_t = cum.T
    dt_t = dtv.T
    wout_t = (jnp.exp(tot - cum) * dtv).T

    lane = lax.broadcasted_iota(jnp.int32, (q, LANES), 1)
    left = lane < SSM_HEADDIM
    hpg = SSM_HEADS // SSM_GROUPS
    for g in range(SSM_GROUPS):
        bm = act[:, D_INNER + g * D_STATE:D_INNER + (g + 1) * D_STATE]
        cm = act[:, D_INNER + (SSM_GROUPS + g) * D_STATE:D_INNER + (SSM_GROUPS + g + 1) * D_STATE]
        bm_t = bm.T
        cm_b = cm.astype(BF16)
        cb = lax.dot_general(cm_b, bm.astype(BF16), (((1,), (1,)), ((), ())),
                             preferred_element_type=F32)
        for pp in range(hpg // 2):
            p = g * (hpg // 2) + pp
            xp = act[:, p * LANES:(p + 1) * LANES]
            xbd = jnp.concatenate([jnp.where(left, xp, 0.0), jnp.where(left, 0.0, xp)], axis=0).astype(BF16)
            ms, bs = [], []
            for h in (2 * p, 2 * p + 1):
                seg = cum[:, h:h + 1] - cum_t[h:h + 1, :]
                lm = jnp.exp(jnp.where(keep, seg, -jnp.inf))
                ms.append((cb * lm * dt_t[h:h + 1, :]).astype(BF16))
                bs.append((bm_t * wout_t[h:h + 1, :]).astype(BF16))
            y = jnp.dot(jnp.concatenate(ms, axis=1), xbd, preferred_element_type=F32)
            st = state[p]
            y_off = jnp.dot(cm_b, st.astype(BF16), preferred_element_type=F32)
            e2 = jnp.where(left, ecum[:, 2 * p:2 * p + 1], ecum[:, 2 * p + 1:2 * p + 2])
            y_ref[:, p * LANES:(p + 1) * LANES] = y + y_off * e2 + dskip_ref[:, p * LANES:(p + 1) * LANES] * xp
            et2 = jnp.where(left[0:1], etot[:, 2 * p:2 * p + 1], etot[:, 2 * p + 1:2 * p + 2])
            state[p] = st * et2 + jnp.dot(jnp.concatenate(bs, axis=1), xbd, preferred_element_type=F32)

    if emit_state:
        @pl.when(c == nc - 1)
        def _():
            st_ref[...] = state[...]


def _ssd_scan(xbc, dt, grp, direction, conv_w, conv_b, dt_bias, a_log, d_skip, h0, emit_state):
    t = xbc.shape[0]
    q = CHUNK
    nc = grp.length // q
    reverse = direction == 1
    zero_init = h0 is None
    boff = grp.row0 // q
    npairs = SSM_HEADS // 2

    def ce(c):
        return (nc - 1 - c) if reverse else c

    cur_map = lambda b, c: (boff + b * nc + ce(c), 0)
    prev_map = lambda b, c: (jnp.maximum((boff + b * nc + ce(c)) * (q // SUBLANES) - 1, 0), 0)
    next_map = lambda b, c: (jnp.minimum((boff + b * nc + ce(c) + 1) * (q // SUBLANES), t // SUBLANES - 1), 0)
    in_specs = [
        pl.BlockSpec((q, CONV_DIM), cur_map),
        pl.BlockSpec((SUBLANES, CONV_DIM), prev_map),
        pl.BlockSpec((SUBLANES, CONV_DIM), next_map),
        pl.BlockSpec((q, LANES), lambda b, c: (boff + b * nc + ce(c), direction)),
        _const_spec((SUBLANES, CONV_DIM)), _const_spec((1, CONV_DIM)),
        _const_spec((1, LANES)), _const_spec((1, LANES)), _const_spec((1, D_INNER)),
    ]
    args = [xbc, xbc, xbc, dt, conv_w, conv_b, dt_bias, a_log, d_skip]
    if not zero_init:
        in_specs.append(pl.BlockSpec((None, npairs, D_STATE, LANES), lambda b, c: (b, 0, 0, 0)))
        args.append(h0)
    out_specs = [pl.BlockSpec((q, D_INNER), lambda b, c: (b * nc + ce(c), 0))]
    out_shape = [jax.ShapeDtypeStruct((grp.nb * grp.length, D_INNER), F32)]
    if emit_state:
        out_specs.append(pl.BlockSpec((None, npairs, D_STATE, LANES), lambda b, c: (b, 0, 0, 0)))
        out_shape.append(jax.ShapeDtypeStruct((grp.nb, npairs, D_STATE, LANES), F32))
    res = pl.pallas_call(
        functools.partial(_ssd_kernel, reverse=reverse, zero_init=zero_init, emit_state=emit_state, nc=nc),
        grid=(grp.nb, nc),
        in_specs=in_specs,
        out_specs=out_specs,
        out_shape=out_shape,
        scratch_shapes=[pltpu.VMEM((npairs, D_STATE, LANES), F32),
                        pltpu.VMEM((q + 2 * SUBLANES, CONV_DIM), F32),
                        pltpu.VMEM((q, CONV_DIM), F32)],
        compiler_params=_cparams(("arbitrary", "arbitrary")),
        name="ssd_scan",
    )(*args)
    return res if emit_state else (res[0], None)


def _ssm_out_kernel(yfp_ref, yfs_ref, ybp_ref, ybs_ref, z_ref, ng_ref, w_ref, x_ref, mod_ref, o_ref,
                    *, n_prompt_tiles):
    is_prompt = pl.program_id(0) < n_prompt_tiles
    y = jnp.where(is_prompt, yfp_ref[...] + ybp_ref[...], yfs_ref[...] + ybs_ref[...])
    y = y * _silu(z_ref[...])
    yn = _rmsnorm(y, ng_ref[...]).astype(BF16)
    o_ref[...] = x_ref[...] + mod_ref[...][2:3] * jnp.dot(yn, w_ref[...], preferred_element_type=F32)


def _pair_state(h):
    b = h.shape[0]
    return h.reshape(b, SSM_HEADS // 2, 2, SSM_HEADDIM, D_STATE).transpose(0, 1, 4, 2, 3).reshape(
        b, SSM_HEADS // 2, D_STATE, 2 * SSM_HEADDIM)


def _unpair_state(s):
    b = s.shape[0]
    return s.reshape(b, SSM_HEADS // 2, D_STATE, 2, SSM_HEADDIM).transpose(0, 1, 3, 4, 2).reshape(
        b, SSM_HEADS, SSM_HEADDIM, D_STATE)


def _ssm_layer(x, modt, groups, slot, g, h0_sample, w_in, conv_w, conv_b, dt_bias, a_log, d_skip, norm_g, w_out):
    t = x.shape[0]
    tm = TOKEN_TILE
    wz = w_in[:, :D_INNER].astype(BF16)
    wx = w_in[:, D_INNER:D_INNER + CONV_DIM].astype(BF16)
    wdt_raw = w_in[:, D_INNER + CONV_DIM:].reshape(-1, 2, SSM_HEADS)
    lane_pad = lambda v: jnp.pad(v, [(0, 0)] * (v.ndim - 1) + [(0, LANES - v.shape[-1])])
    wdt = lane_pad(wdt_raw).reshape(-1, 2 * LANES).astype(BF16)
    row = lambda c: pl.BlockSpec((tm, c), lambda i: (i, 0))
    z, xbc, dt = pl.pallas_call(
        _ssm_proj_kernel,
        grid=(t // tm,),
        in_specs=[row(D_MODEL), _mod_spec(groups, tm, slot), _const_spec((1, D_MODEL)),
                  _const_spec(wz.shape), _const_spec(wx.shape), _const_spec(wdt.shape)],
        out_specs=[row(D_INNER), row(CONV_DIM), row(2 * LANES)],
        out_shape=[jax.ShapeDtypeStruct((t, D_INNER), F32), jax.ShapeDtypeStruct((t, CONV_DIM), F32),
                   jax.ShapeDtypeStruct((t, 2 * LANES), F32)],
        compiler_params=_cparams(("arbitrary",)),
        name="ssm_project",
    )(x, modt, g.reshape(1, D_MODEL), wz, wx, wdt)

    cw = jnp.pad(conv_w, ((0, SUBLANES - CONV_W), (0, 0)))
    cb = conv_b.reshape(1, CONV_DIM)
    prompt, sample = groups
    ys = {}
    states = []
    for d in range(2):
        dtb = lane_pad(dt_bias[d]).reshape(1, LANES)
        alog = lane_pad(a_log[d]).reshape(1, LANES)
        dsk = jnp.repeat(d_skip[d], SSM_HEADDIM).reshape(1, D_INNER)
        yp, st = _ssd_scan(xbc, dt, prompt, d, cw, cb, dtb, alog, dsk, None, True)
        ysamp, _ = _ssd_scan(xbc, dt, sample, d, cw, cb, dtb, alog, dsk, _pair_state(h0_sample[:, d]), False)
        ys[d] = [yp, ysamp]
        states.append(_unpair_state(st))

    to = TOKEN_TILE // 2
    row = lambda c: pl.BlockSpec((to, c), lambda i: (i, 0))
    y_specs, n_p = _two_group_specs(groups, to, D_INNER)
    x = pl.pallas_call(
        functools.partial(_ssm_out_kernel, n_prompt_tiles=n_p),
        grid=(t // to,),
        in_specs=y_specs + y_specs + [row(D_INNER), _const_spec((1, D_INNER)),
                  _const_spec((D_INNER, D_MODEL)), row(D_MODEL), _mod_spec(groups, to, slot)],
        out_specs=row(D_MODEL),
        out_shape=jax.ShapeDtypeStruct((t, D_MODEL), F32),
        compiler_params=_cparams(("arbitrary",)),
        name="ssm_out",
    )(*ys[0], *ys[1], z, norm_g.reshape(1, D_INNER), w_out.astype(BF16), x, modt)
    return x, jnp.stack(states, axis=1)


def kernel(x_prompt, x_sample, cache_mla, state_ssm, c, c_ctx, mod_w, mod_b, norm_g, ffn_w_in, ffn_w_out,
           mla_w_in, mla_q_norm, mla_kv_norm, mla_wq_b, mla_wkv_b, mla_wo, ssm_w_in, ssm_conv_w, ssm_conv_b,
           ssm_dt_bias, ssm_a_log, ssm_d, ssm_norm_g, ssm_w_out, final_norm_g):
    nbp, lp, d = x_prompt.shape
    nbs, ls, _ = x_sample.shape
    prompt = _Group(0, nbp, lp)
    sample = _Group(nbp * lp, nbs, ls)
    groups = (prompt, sample)
    x = jnp.concatenate([x_prompt.reshape(-1, d), x_sample.reshape(-1, d)], axis=0)
    t = x.shape[0]

    ncond = 1 + nbs
    sc = jnp.concatenate([c_ctx[None, :], c, jnp.zeros((SUBLANES - ncond, d), F32)], axis=0)
    mod = _modulation(sc, mod_w, mod_b)
    modt = mod[:, :ncond].reshape(DEPTH, ncond, 3, 3, d).transpose(1, 0, 2, 3, 4).reshape(ncond, DEPTH * 3, 3, d)

    cos, sin = _rope_tables(groups, t)
    new_mla, new_ssm = [], []
    for i in range(DEPTH):
        j = i // 2
        x = _ffn_half(x, modt, groups, 3 * i, norm_g[i, 0], ffn_w_in[i, 0].astype(BF16),
                      ffn_w_out[i, 0].astype(BF16), final_norm_g, False)
        if i % 2 == 0:
            x, entry = _mla_layer(x, modt, groups, 3 * i + 1, norm_g[i, 1], cache_mla[:, j], cos, sin,
                                  mla_w_in[j], mla_q_norm[j], mla_kv_norm[j], mla_wq_b[j], mla_wkv_b[j], mla_wo[j])
            new_mla.append(entry)
        else:
            x, st = _ssm_layer(x, modt, groups, 3 * i + 1, norm_g[i, 1], state_ssm[:, j], ssm_w_in[j],
                               ssm_conv_w[j], ssm_conv_b[j], ssm_dt_bias[j], ssm_a_log[j], ssm_d[j],
                               ssm_norm_g[j], ssm_w_out[j])
            new_ssm.append(st)
        x = _ffn_half(x, modt, groups, 3 * i + 2, norm_g[i, 2], ffn_w_in[i, 1].astype(BF16),
                      ffn_w_out[i, 1].astype(BF16), final_norm_g, i == DEPTH - 1)
    y_prompt = x[:nbp * lp].reshape(nbp, lp, d)
    y_sample = x[nbp * lp:].reshape(nbs, ls, d)
    return y_prompt, y_sample, jnp.stack(new_mla, axis=1), jnp.stack(new_ssm, axis=1)
```

```python
import functools
import math

import jax
import jax.numpy as jnp
from jax import lax
from jax.experimental import pallas as pl
from jax.experimental.pallas import tpu as pltpu

F32 = jnp.float32
BF16 = jnp.bfloat16

D_MODEL = 1024
DEPTH = 4
N_MOD = 9
D_FF = 2816
GRID_W = 64
MLA_HEADS = 16
Q_LORA = 512
KV_LORA = 256
QK_NOPE = 64
QK_ROPE = 32
V_HEAD = 64
QK_HEAD = QK_NOPE + QK_ROPE
ROPE_BASE = 10000.0
HEAD_PAD = 128
VT_ROWS = 80
D_INNER = 2048
SSM_HEADDIM = 64
SSM_HEADS = 32
SSM_GROUPS = 4
D_STATE = 128
CONV_W = 5
CONV_DIM = D_INNER + 2 * SSM_GROUPS * D_STATE
CHUNK = 128
EPS = 1e-6
LOG2E = math.log2(math.e)

LANES = 128
SUBLANES = 8
TOKEN_TILE = 512
FF_CHUNK = 256
VMEM_LIMIT = 56 * 1024 * 1024

NT = (((1,), (1,)), ((), ()))
TN = (((0,), (0,)), ((), ()))


def _cparams(sem):
    return pltpu.CompilerParams(dimension_semantics=sem, vmem_limit_bytes=VMEM_LIMIT)


def _const_spec(shape, index=None):
    index = tuple(index or ())
    block = (None,) * len(index) + tuple(shape)
    zeros = (0,) * len(shape)
    return pl.BlockSpec(block, lambda *_: index + zeros, pipeline_mode=pl.Buffered(1))


def _rmsnorm(x, g):
    ms = jnp.mean(x * x, axis=-1, keepdims=True)
    return (x * lax.rsqrt(ms + EPS)) * g


def _silu(x):
    h = 0.5 * x
    return h * jnp.tanh(h) + h


def _adaln(x, g, shift, scale):
    return _rmsnorm(x, g) * (1.0 + scale) + shift


class _Group:
    def __init__(self, row0, nb, length):
        self.row0, self.nb, self.length = row0, nb, length

    @property
    def rows(self):
        return self.nb * self.length


def _mod_kernel(sc_ref, w_ref, b_ref, o_ref):
    s = _silu(sc_ref[...])
    o_ref[...] = jnp.dot(s.astype(BF16), w_ref[...].astype(BF16),
                         preferred_element_type=F32) + b_ref[...]


def _modulation(sc, mod_w, mod_b):
    tn = 1024
    n = N_MOD * D_MODEL
    return pl.pallas_call(
        _mod_kernel,
        grid=(DEPTH, n // tn),
        in_specs=[
            pl.BlockSpec((SUBLANES, D_MODEL), lambda i, j: (0, 0)),
            pl.BlockSpec((None, D_MODEL, tn), lambda i, j: (i, 0, j)),
            pl.BlockSpec((None, 1, tn), lambda i, j: (i, 0, j)),
        ],
        out_specs=pl.BlockSpec((None, SUBLANES, tn), lambda i, j: (i, 0, j)),
        out_shape=jax.ShapeDtypeStruct((DEPTH, SUBLANES, n), F32),
        compiler_params=_cparams(("arbitrary", "arbitrary")),
        name="modulation",
    )(sc, mod_w, mod_b.reshape(DEPTH, 1, n))


def _mod_spec(groups, tile, slot, first_tile=0):
    bounds = []
    for g in groups:
        for b in range(g.nb if g.row0 else 1):
            bounds.append((g.row0 + b * g.length) // tile)

    def index_map(i, *_):
        gidx = 0
        for lo in bounds[1:]:
            gidx = gidx + (i + first_tile >= lo).astype(jnp.int32)
        return (gidx, slot, 0, 0)

    return pl.BlockSpec((None, None, 3, D_MODEL), index_map)


def _two_group_specs(groups, tile, width, axis=0):
    prompt, sample = groups
    n_p = prompt.rows // tile
    n_s = sample.rows // tile
    if axis == 0:
        return [pl.BlockSpec((tile, width), lambda i: (jnp.minimum(i, n_p - 1), 0)),
                pl.BlockSpec((tile, width), lambda i: (jnp.clip(i - n_p, 0, n_s - 1), 0))], n_p
    return [pl.BlockSpec((width, tile), lambda i: (0, jnp.minimum(i, n_p - 1))),
            pl.BlockSpec((width, tile), lambda i: (0, jnp.clip(i - n_p, 0, n_s - 1)))], n_p


def _ffn_kernel(x_ref, mod_ref, g_ref, win_ref, wout_ref, fg_ref, o_ref, a_scr, *, final_norm):
    x = x_ref[...]
    m = mod_ref[...]
    hb = _adaln(x, g_ref[...], m[0:1], m[1:2]).astype(BF16)
    for c in range(D_FF // FF_CHUNK):
        lo = c * FF_CHUNK
        gate = jnp.dot(hb, win_ref[:, lo:lo + FF_CHUNK], preferred_element_type=F32)
        up = jnp.dot(hb, win_ref[:, D_FF + lo:D_FF + lo + FF_CHUNK], preferred_element_type=F32)
        a_scr[:, lo:lo + FF_CHUNK] = (_silu(gate) * up).astype(BF16)
    y = jnp.dot(a_scr[...], wout_ref[...], preferred_element_type=F32)
    out = x + (0.5 * m[2:3]) * y
    if final_norm:
        out = _rmsnorm(out, fg_ref[...])
    o_ref[...] = out


def _ffn_half(x, modt, groups, layer, half, norm_g3, w_in, w_out, final_g, final_norm=False, tiles=None):
    tm = TOKEN_TILE
    first, count = tiles or (0, x.shape[0] // tm)
    slot = 3 * layer + 2 * half
    return pl.pallas_call(
        functools.partial(_ffn_kernel, final_norm=final_norm),
        grid=(count,),
        in_specs=[
            pl.BlockSpec((tm, D_MODEL), lambda i: (i + first, 0)),
            _mod_spec(groups, tm, slot, first),
            _const_spec((1, D_MODEL), (slot,)),
            _const_spec((D_MODEL, 2 * D_FF), (layer, half)),
            _const_spec((D_FF, D_MODEL), (layer, half)),
            _const_spec((1, D_MODEL)),
        ],
        out_specs=pl.BlockSpec((tm, D_MODEL), lambda i: (i, 0)),
        out_shape=jax.ShapeDtypeStruct((count * tm, D_MODEL), F32),
        scratch_shapes=[pltpu.VMEM((tm, D_FF), BF16)],
        compiler_params=_cparams(("arbitrary",)),
        name="ffn_half",
    )(x, modt, norm_g3, w_in, w_out, final_g.reshape(1, D_MODEL))


def _expand_kv(ckv_b, krope, wk_ref, wvt_ref, k_ref, vt_ref):
    kn = jnp.dot(ckv_b, wk_ref[...], preferred_element_type=F32)
    for h in range(MLA_HEADS):
        lo = h * HEAD_PAD
        k_ref[:, lo:lo + HEAD_PAD] = (kn[:, lo:lo + HEAD_PAD] + krope).astype(BF16)
    vt_ref[...] = lax.dot_general(wvt_ref[...], ckv_b, NT, preferred_element_type=F32).astype(BF16)
    ones = jnp.ones((VT_ROWS - V_HEAD, ckv_b.shape[0]), BF16)
    for h in range(MLA_HEADS):
        vt_ref[h * VT_ROWS + V_HEAD:(h + 1) * VT_ROWS, :] = ones


def _mla_proj_kernel(x_ref, mod_ref, g_ref, win_ref, qn_ref, kvn_ref, wq1_ref, wq2_ref,
                     wk_ref, wvt_ref, cos_ref, sin_ref,
                     q_ref, k_ref, vt_ref, ckv_ref, kr_ref):
    x = x_ref[...]
    m = mod_ref[...]
    hb = _adaln(x, g_ref[...], m[0:1], m[1:2]).astype(BF16)
    a = jnp.dot(hb, win_ref[...], preferred_element_type=F32)
    qa = _rmsnorm(a[:, :Q_LORA], qn_ref[...]).astype(BF16)
    ckv = _rmsnorm(a[:, Q_LORA:Q_LORA + KV_LORA], kvn_ref[...])
    kr = a[:, Q_LORA + KV_LORA:Q_LORA + KV_LORA + HEAD_PAD]
    kr_rot = a[:, Q_LORA + KV_LORA + HEAD_PAD:]
    cosk = cos_ref[...]
    sink = sin_ref[...]
    lane = lax.broadcasted_iota(jnp.int32, cosk.shape, 1)
    scale = LOG2E / math.sqrt(QK_HEAD)
    cosq = (cosk + (lane < QK_NOPE).astype(F32)) * scale
    sinq = sink * scale
    q1 = jnp.dot(qa, wq1_ref[...], preferred_element_type=F32)
    q2 = jnp.dot(qa, wq2_ref[...], preferred_element_type=F32)
    for h in range(MLA_HEADS):
        lo = h * HEAD_PAD
        q_ref[:, lo:lo + HEAD_PAD] = (q1[:, lo:lo + HEAD_PAD] * cosq
                                      + q2[:, lo:lo + HEAD_PAD] * sinq).astype(BF16)
    ckv_ref[...] = ckv
    kr_ref[...] = kr
    _expand_kv(ckv.astype(BF16), kr * cosk + kr_rot * sink, wk_ref, wvt_ref, k_ref, vt_ref)


def _mla_ctx_kernel(ckv_ref, kr_ref, wk_ref, wvt_ref, k_ref, vt_ref):
    _expand_kv(ckv_ref[...].astype(BF16), kr_ref[...], wk_ref, wvt_ref, k_ref, vt_ref)


def _mla_attn_kernel(*refs, n_parts, heads):
    q_ref = refs[0]
    kv_refs = refs[1:1 + 2 * n_parts]
    o_ref = refs[1 + 2 * n_parts]
    def scores(h):
        qh = q_ref[:, h * HEAD_PAD:(h + 1) * HEAD_PAD]
        return [lax.dot_general(kv_refs[2 * i][:, h * HEAD_PAD:(h + 1) * HEAD_PAD], qh, NT,
                                preferred_element_type=F32) for i in range(n_parts)]

    ss_next = scores(0)
    for h in range(heads):
        ss = ss_next
        if h + 1 < heads:
            ss_next = scores(h + 1)
        mx = ss[0].max(axis=0, keepdims=True)
        for s in ss[1:]:
            mx = jnp.maximum(mx, s.max(axis=0, keepdims=True))
        acc = None
        for i, s in enumerate(ss):
            p = jnp.exp2(s - mx).astype(BF16)
            o = jnp.dot(kv_refs[2 * i + 1][h * VT_ROWS:(h + 1) * VT_ROWS, :], p, preferred_element_type=F32)
            acc = o if acc is None else acc + o
        o_ref[h * V_HEAD:(h + 1) * V_HEAD, :] = (acc[:V_HEAD] / acc[V_HEAD:V_HEAD + 1]).astype(BF16)


def _mla_attention(q, parts, grp, tq, heads):
    nq = grp.length // tq
    qoff = grp.row0 // tq
    in_specs = [pl.BlockSpec((tq, heads * HEAD_PAD), lambda b, p, i: (qoff + b * nq + i, p))]
    args = [q]
    for k, vt, row0, lk in parts:
        koff = row0 // lk
        in_specs.append(pl.BlockSpec((lk, heads * HEAD_PAD), lambda b, p, i, koff=koff: (koff + b, p)))
        in_specs.append(pl.BlockSpec((heads * VT_ROWS, lk), lambda b, p, i, koff=koff: (p, koff + b)))
        args += [k, vt]
    return pl.pallas_call(
        functools.partial(_mla_attn_kernel, n_parts=len(parts), heads=heads),
        grid=(grp.nb, MLA_HEADS // heads, nq),
        in_specs=in_specs,
        out_specs=pl.BlockSpec((heads * V_HEAD, tq), lambda b, p, i: (p, b * nq + i)),
        out_shape=jax.ShapeDtypeStruct((MLA_HEADS * V_HEAD, grp.rows), BF16),
        compiler_params=_cparams(("arbitrary", "arbitrary", "arbitrary")),
        name="mla_attention",
    )(*args)


def _mla_out_kernel(otp_ref, ots_ref, wo_ref, x_ref, mod_ref, o_ref, *, n_prompt_tiles):
    o_t = jnp.where(pl.program_id(0) < n_prompt_tiles, otp_ref[...], ots_ref[...])
    y = lax.dot_general(o_t, wo_ref[...], TN, preferred_element_type=F32)
    o_ref[...] = x_ref[...] + mod_ref[...][2:3] * y


def _rope_tables(groups):
    prompt, sample = groups
    half = QK_ROPE // 2
    freqs = 1.0 / (ROPE_BASE ** (jnp.arange(0, half, 2, dtype=F32) / half))
    pos_i = jnp.arange(sample.length)
    pos = jnp.stack([pos_i // GRID_W, pos_i % GRID_W], axis=-1).astype(F32)
    ang = pos[:, :, None] * freqs
    ang = jnp.broadcast_to(ang[:, :, None, :], (sample.length, 2, 2, half // 2)).reshape(sample.length, QK_ROPE)
    lanes = lambda v: jnp.pad(v, ((0, 0), (QK_NOPE, HEAD_PAD - QK_HEAD)))
    cos = jnp.concatenate([lanes(jnp.ones((prompt.rows, QK_ROPE), F32))] + [lanes(jnp.cos(ang))] * sample.nb, axis=0)
    sin = jnp.concatenate([jnp.zeros((prompt.rows, HEAD_PAD), F32)] + [lanes(jnp.sin(ang))] * sample.nb, axis=0)
    return cos, sin


def _rot_cols(w):
    q = QK_ROPE // 4
    w4 = w.reshape(w.shape[:-1] + (2, 2, q))
    return jnp.stack([-w4[..., 1, :], w4[..., 0, :]], axis=-2).reshape(w.shape)


def _mla_weights(w_in, wq_b, wkv_b):
    d = w_in.shape[0]
    z = lambda *s: jnp.zeros(s, F32)
    k_r = w_in[:, Q_LORA + KV_LORA:]
    pad_tile = lambda c: jnp.concatenate([z(d, QK_NOPE), c, z(d, HEAD_PAD - QK_HEAD)], axis=1)
    w_in_ext = jnp.concatenate([w_in[:, :Q_LORA + KV_LORA], pad_tile(k_r), pad_tile(_rot_cols(k_r))], axis=1)
    wq = wq_b.reshape(Q_LORA, MLA_HEADS, QK_HEAD)
    zq = lambda n: z(Q_LORA, MLA_HEADS, n)
    wq1 = jnp.concatenate([wq, zq(HEAD_PAD - QK_HEAD)], axis=-1)
    wq2 = jnp.concatenate([zq(QK_NOPE), _rot_cols(wq[..., QK_NOPE:]), zq(HEAD_PAD - QK_HEAD)], axis=-1)
    wkv = wkv_b.reshape(KV_LORA, MLA_HEADS, QK_NOPE + V_HEAD)
    wk = jnp.concatenate([wkv[..., :QK_NOPE], z(KV_LORA, MLA_HEADS, HEAD_PAD - QK_NOPE)], axis=-1)
    wv = jnp.concatenate([wkv[..., QK_NOPE:], z(KV_LORA, MLA_HEADS, VT_ROWS - V_HEAD)], axis=-1)
    wvt = wv.reshape(KV_LORA, MLA_HEADS * VT_ROWS).T
    flat = lambda w: w.reshape(w.shape[0], -1).astype(BF16)
    return w_in_ext.astype(BF16), flat(wq1), flat(wq2), flat(wk), wvt.astype(BF16)


def _mla_layer(x, modt, groups, layer, norm_g3, cache, cos, sin, w_in, q_norm, kv_norm, wq_b, wkv_b, wo):
    t = x.shape[0]
    tm = TOKEN_TILE
    slot = 3 * layer + 1
    w_in_ext, wq1, wq2, wk, wvt = _mla_weights(w_in, wq_b, wkv_b)
    hd = MLA_HEADS * HEAD_PAD
    vr = MLA_HEADS * VT_ROWS
    row = lambda c: pl.BlockSpec((tm, c), lambda i: (i, 0))
    q, k, vt, ckv, kr = pl.pallas_call(
        _mla_proj_kernel,
        grid=(t // tm,),
        in_specs=[
            row(D_MODEL), _mod_spec(groups, tm, slot), _const_spec((1, D_MODEL), (slot,)),
            _const_spec(w_in_ext.shape), _const_spec((1, Q_LORA)), _const_spec((1, KV_LORA)),
            _const_spec(wq1.shape), _const_spec(wq2.shape), _const_spec(wk.shape), _const_spec(wvt.shape),
            row(HEAD_PAD), row(HEAD_PAD),
        ],
        out_specs=[row(hd), row(hd), pl.BlockSpec((vr, tm), lambda i: (0, i)), row(KV_LORA), row(HEAD_PAD)],
        out_shape=[jax.ShapeDtypeStruct((t, hd), BF16), jax.ShapeDtypeStruct((t, hd), BF16),
                   jax.ShapeDtypeStruct((vr, t), BF16), jax.ShapeDtypeStruct((t, KV_LORA), F32),
                   jax.ShapeDtypeStruct((t, HEAD_PAD), F32)],
        compiler_params=_cparams(("arbitrary",)),
        name="mla_project",
    )(x, modt, norm_g3, w_in_ext, q_norm.reshape(1, -1), kv_norm.reshape(1, -1),
      wq1, wq2, wk, wvt, cos, sin)

    prompt, sample = groups
    nb, past, _ = cache.shape
    cflat = cache.reshape(nb * past, -1)
    ckv_c = cflat[:, :KV_LORA]
    kr_c = jnp.pad(cflat[:, KV_LORA:], ((0, 0), (QK_NOPE, HEAD_PAD - QK_HEAD)))
    crow = lambda c: pl.BlockSpec((past, c), lambda i: (i, 0))
    k_c, vt_c = pl.pallas_call(
        _mla_ctx_kernel,
        grid=(nb,),
        in_specs=[crow(KV_LORA), crow(HEAD_PAD), _const_spec(wk.shape), _const_spec(wvt.shape)],
        out_specs=[crow(hd), pl.BlockSpec((vr, past), lambda i: (0, i))],
        out_shape=[jax.ShapeDtypeStruct((nb * past, hd), BF16), jax.ShapeDtypeStruct((vr, nb * past), BF16)],
        compiler_params=_cparams(("arbitrary",)),
        name="mla_ctx_expand",
    )(ckv_c, kr_c, wk, wvt)

    ot_p = _mla_attention(q, [(k, vt, prompt.row0, prompt.length)], prompt, 256, MLA_HEADS)
    ot_s = _mla_attention(q, [(k_c, vt_c, 0, past), (k, vt, sample.row0, sample.length)], sample, 512, 4)
    vd = MLA_HEADS * V_HEAD
    o_specs, n_p = _two_group_specs(groups, tm, vd, axis=1)
    x = pl.pallas_call(
        functools.partial(_mla_out_kernel, n_prompt_tiles=n_p),
        grid=(t // tm,),
        in_specs=o_specs + [_const_spec((vd, D_MODEL)), row(D_MODEL), _mod_spec(groups, tm, slot)],
        out_specs=row(D_MODEL),
        out_shape=jax.ShapeDtypeStruct((t, D_MODEL), F32),
        compiler_params=_cparams(("arbitrary",)),
        name="mla_out",
    )(ot_p, ot_s, wo.astype(BF16), x, modt)
    entry = jnp.concatenate([ckv[:prompt.rows], kr[:prompt.rows, QK_NOPE:QK_HEAD]], axis=-1)
    return x, entry.reshape(prompt.nb, prompt.length, -1)


def _ssm_proj_kernel(x_ref, mod_ref, g_ref, wz_ref, wx_ref, wdt_ref, z_ref, xbc_ref, dt_ref):
    m = mod_ref[...]
    hb = _adaln(x_ref[...], g_ref[...], m[0:1], m[1:2]).astype(BF16)
    z_ref[...] = jnp.dot(hb, wz_ref[...], preferred_element_type=F32)
    xbc_ref[...] = jnp.dot(hb, wx_ref[...], preferred_element_type=F32)
    dt_ref[...] = jnp.dot(hb, wdt_ref[...], preferred_element_type=F32)


def _split3(a):
    hi = a.astype(BF16)
    r1 = a - hi.astype(F32)
    mid = r1.astype(BF16)
    lo = (r1 - mid.astype(F32)).astype(BF16)
    return hi, mid, lo


def _ssd_kernel(*refs, zero_init, emit_state, has_alias, nc):
    it = iter(refs)
    cur_ref, prev_ref, next_ref, dt_ref = next(it), next(it), next(it), next(it)
    cw_ref, cb_ref, dtb_ref, alog_ref, dskip_ref = next(it), next(it), next(it), next(it), next(it)
    h0_ref = None if zero_init else next(it)
    if has_alias:
        next(it)
    y1_ref, y2_ref = next(it), next(it)
    st_ref = next(it) if emit_state else None
    s_f, s_b, sb_loc, c_keep, ecb_keep, etb_keep, xpad, act = (next(it) for _ in range(8))

    q = CHUNK
    step = pl.program_id(1)
    npairs = SSM_HEADS // 2
    hpg = SSM_HEADS // SSM_GROUPS
    lane = lax.broadcasted_iota(jnp.int32, (q, LANES), 1)
    left = lane < SSM_HEADDIM

    def pair_cols(v, p):
        return jnp.where(left[:v.shape[0]], v[:, 2 * p:2 * p + 1], v[:, 2 * p + 1:2 * p + 2])

    @pl.when(step == 0)
    def _init():
        if zero_init:
            s_f[...] = jnp.zeros(s_f.shape, F32)
            s_b[...] = jnp.zeros(s_b.shape, F32)
        else:
            s_f[...] = h0_ref[0]
            s_b[...] = h0_ref[1]

    @pl.when(step < nc)
    def _sweep_up():
        c = step
        xpad[0:SUBLANES, :] = jnp.where(c > 0, prev_ref[...], 0.0)
        xpad[SUBLANES:SUBLANES + q, :] = cur_ref[...]
        xpad[SUBLANES + q:, :] = jnp.where(c < nc - 1, next_ref[...], 0.0)
        cw = 512
        r0 = SUBLANES - CONV_W // 2
        for j in range(CONV_DIM // cw):
            cols = slice(j * cw, (j + 1) * cw)
            acc = jnp.broadcast_to(cb_ref[:, cols], (q, cw))
            for k in range(CONV_W):
                acc = acc + xpad[r0 + k:r0 + k + q, cols] * cw_ref[k:k + 1, cols]
            act[:, cols] = _silu(acc)
        c_keep[c] = act[:, D_INNER + SSM_GROUPS * D_STATE:].astype(BF16)

        dtr = dt_ref[...] + dtb_ref[...]
        dtv = jnp.maximum(dtr, 0.0) + jnp.log1p(jnp.exp(-jnp.abs(dtr)))
        a2 = dtv * (-LOG2E * jnp.exp(alog_ref[...]))
        ri = lax.broadcasted_iota(jnp.int32, (q, q), 0)
        ci = lax.broadcasted_iota(jnp.int32, (q, q), 1)
        lower = ci <= ri
        slower = ci < ri
        supper = ci > ri
        tot = jnp.sum(a2, axis=0, keepdims=True)
        dirs = []
        for d, keep in enumerate((lower, ci >= ri)):
            ad = a2[:, d * LANES:(d + 1) * LANES]
            dtd = dtv[:, d * LANES:(d + 1) * LANES]
            td = tot[:, d * LANES:(d + 1) * LANES]
            tri = keep.astype(BF16)
            cum = sum(jnp.dot(tri, part, preferred_element_type=F32) for part in _split3(ad))
            dirs.append(dict(cum=cum, cum_t=cum.T, dt_t=dtd.T, ecum=jnp.exp2(cum), etot=jnp.exp2(td),
                             wout_t=(jnp.exp2(td - cum) * dtd).T))
        fw, bw = dirs
        dt_sum_t = fw["dt_t"] + bw["dt_t"]
        ecb_keep[c] = bw["ecum"]
        etb_keep[c] = jnp.broadcast_to(bw["etot"], (SUBLANES, LANES))
        dsum = dskip_ref[0:1, :] + dskip_ref[1:2, :]

        for g in range(SSM_GROUPS):
            bm = act[:, D_INNER + g * D_STATE:D_INNER + (g + 1) * D_STATE]
            cm_b = c_keep[c, :, g * D_STATE:(g + 1) * D_STATE]
            bm_t = bm.T
            cb = lax.dot_general(cm_b, bm.astype(BF16), NT, preferred_element_type=F32)
            for pp in range(hpg // 2):
                p = g * (hpg // 2) + pp
                xp = act[:, p * LANES:(p + 1) * LANES]
                xbd = jnp.concatenate([jnp.where(left, xp, 0.0), jnp.where(left, 0.0, xp)], axis=0).astype(BF16)
                ms, bfs, bbs = [], [], []
                for h in (2 * p, 2 * p + 1):
                    seg = jnp.where(lower, fw["cum"][:, h:h + 1] - fw["cum_t"][h:h + 1, :],
                                    bw["cum"][:, h:h + 1] - bw["cum_t"][h:h + 1, :])
                    wgt = jnp.where(slower, fw["dt_t"][h:h + 1, :],
                                    jnp.where(supper, bw["dt_t"][h:h + 1, :], dt_sum_t[h:h + 1, :]))
                    ms.append((cb * jnp.exp2(seg) * wgt).astype(BF16))
                    bfs.append((bm_t * fw["wout_t"][h:h + 1, :]).astype(BF16))
                    bbs.append((bm_t * bw["wout_t"][h:h + 1, :]).astype(BF16))
                y = jnp.dot(jnp.concatenate(ms, axis=1), xbd, preferred_element_type=F32)
                st = s_f[p]
                y_off = jnp.dot(cm_b, st.astype(BF16), preferred_element_type=F32)
                y1_ref[:, p * LANES:(p + 1) * LANES] = (y + y_off * pair_cols(fw["ecum"], p)
                                                        + dsum[:, p * LANES:(p + 1) * LANES] * xp)
                s_f[p] = st * pair_cols(fw["etot"], p) + jnp.dot(jnp.concatenate(bfs, axis=1), xbd,
                                                                 preferred_element_type=F32)
                sb_loc[c, p] = jnp.dot(jnp.concatenate(bbs, axis=1), xbd, preferred_element_type=F32)

    @pl.when(step >= nc)
    def _sweep_down():
        c = 2 * nc - 1 - step
        ecb = ecb_keep[c]
        etb = etb_keep[c][0:1]
        for p in range(npairs):
            g = p // (hpg // 2)
            st = s_b[p]
            y_off = jnp.dot(c_keep[c, :, g * D_STATE:(g + 1) * D_STATE], st.astype(BF16),
                            preferred_element_type=F32)
            y2_ref[:, p * LANES:(p + 1) * LANES] = y_off * pair_cols(ecb, p)
            s_b[p] = st * pair_cols(etb, p) + sb_loc[c, p]

        if emit_state:
            @pl.when(step == 2 * nc - 1)
            def _emit():
                for p in range(npairs):
                    st_ref[0, p] = s_f[p].T
                    st_ref[1, p] = s_b[p].T


def _ssd_scan(xbc, dt, grp, conv_w, conv_b, dt_bias, a_log, d_skip, h0, state_out):
    t = xbc.shape[0]
    q = CHUNK
    nc = grp.length // q
    zero_init = h0 is None
    emit_state = state_out is not None
    boff = grp.row0 // q
    npairs = SSM_HEADS // 2
    qs = q // SUBLANES

    chunk = lambda b, s: boff + b * nc + jnp.minimum(s, nc - 1)
    in_specs = [
        pl.BlockSpec((q, CONV_DIM), lambda b, s: (chunk(b, s), 0)),
        pl.BlockSpec((SUBLANES, CONV_DIM), lambda b, s: (jnp.maximum(chunk(b, s) * qs - 1, 0), 0)),
        pl.BlockSpec((SUBLANES, CONV_DIM), lambda b, s: (jnp.minimum((chunk(b, s) + 1) * qs, t // SUBLANES - 1), 0)),
        pl.BlockSpec((q, 2 * LANES), lambda b, s: (chunk(b, s), 0)),
        _const_spec((SUBLANES, CONV_DIM)), _const_spec((1, CONV_DIM)),
        _const_spec((1, 2 * LANES)), _const_spec((1, 2 * LANES)), _const_spec((2, D_INNER)),
    ]
    args = [xbc, xbc, xbc, dt, conv_w, conv_b, dt_bias, a_log, d_skip]
    if not zero_init:
        in_specs.append(pl.BlockSpec((None, 2, npairs, D_STATE, LANES), lambda b, s: (b, 0, 0, 0, 0)))
        args.append(h0)
    aliases = {}
    has_alias = emit_state and state_out[0] is not None
    if has_alias:
        in_specs.append(pl.BlockSpec(memory_space=pl.ANY))
        args.append(state_out[0])
        aliases = {len(args) - 1: 2}
    out_specs = [
        pl.BlockSpec((q, D_INNER), lambda b, s: (b * nc + jnp.minimum(s, nc - 1), 0)),
        pl.BlockSpec((q, D_INNER), lambda b, s: (b * nc + jnp.where(s < nc, nc - 1, 2 * nc - 1 - s), 0)),
    ]
    out_shape = [jax.ShapeDtypeStruct((grp.rows, D_INNER), F32)] * 2
    if emit_state:
        layer = state_out[1]
        n_layers = state_out[2]
        out_specs.append(pl.BlockSpec((None, None, 2, npairs, LANES, D_STATE),
                                      lambda b, s: (b, layer, 0, 0, 0, 0)))
        out_shape.append(jax.ShapeDtypeStruct((grp.nb, n_layers, 2, npairs, LANES, D_STATE), F32))
    return pl.pallas_call(
        functools.partial(_ssd_kernel, zero_init=zero_init, emit_state=emit_state, has_alias=has_alias, nc=nc),
        grid=(grp.nb, 2 * nc),
        in_specs=in_specs,
        out_specs=out_specs,
        out_shape=out_shape,
        scratch_shapes=[pltpu.VMEM((npairs, D_STATE, LANES), F32),
                        pltpu.VMEM((npairs, D_STATE, LANES), F32),
                        pltpu.VMEM((nc, npairs, D_STATE, LANES), F32),
                        pltpu.VMEM((nc, q, SSM_GROUPS * D_STATE), BF16),
                        pltpu.VMEM((nc, q, LANES), F32),
                        pltpu.VMEM((nc, SUBLANES, LANES), F32),
                        pltpu.VMEM((q + 2 * SUBLANES, CONV_DIM), F32),
                        pltpu.VMEM((q, CONV_DIM), F32)],
        input_output_aliases=aliases,
        compiler_params=_cparams(("arbitrary", "arbitrary")),
        name="ssd_scan",
    )(*args)


def _ssm_out_kernel(y1p_ref, y1s_ref, y2p_ref, y2s_ref, z_ref, ng_ref, w_ref, x_ref, mod_ref, o_ref,
                    *, n_prompt_tiles):
    is_prompt = pl.program_id(0) < n_prompt_tiles
    y = jnp.where(is_prompt, y1p_ref[...] + y2p_ref[...], y1s_ref[...] + y2s_ref[...])
    y = y * _silu(z_ref[...])
    yn = _rmsnorm(y, ng_ref[...]).astype(BF16)
    o_ref[...] = x_ref[...] + mod_ref[...][2:3] * jnp.dot(yn, w_ref[...], preferred_element_type=F32)


def _pair_state(h):
    b = h.shape[0]
    return h.reshape(b, 2, SSM_HEADS // 2, 2, SSM_HEADDIM, D_STATE).transpose(0, 1, 2, 5, 3, 4).reshape(
        b, 2, SSM_HEADS // 2, D_STATE, 2 * SSM_HEADDIM)


def _ssm_layer(x, modt, groups, layer, norm_g3, h0_sample, state_out, w_z, w_x, w_dt, conv_w, conv_b,
               dt_bias, a_log, d_skip, norm_g, w_out):
    t = x.shape[0]
    tm = TOKEN_TILE
    slot = 3 * layer + 1
    row = lambda c: pl.BlockSpec((tm, c), lambda i: (i, 0))
    z, xbc, dt = pl.pallas_call(
        _ssm_proj_kernel,
        grid=(t // tm,),
        in_specs=[row(D_MODEL), _mod_spec(groups, tm, slot), _const_spec((1, D_MODEL), (slot,)),
                  _const_spec(w_z.shape), _const_spec(w_x.shape), _const_spec(w_dt.shape)],
        out_specs=[row(D_INNER), row(CONV_DIM), row(2 * LANES)],
        out_shape=[jax.ShapeDtypeStruct((t, D_INNER), F32), jax.ShapeDtypeStruct((t, CONV_DIM), F32),
                   jax.ShapeDtypeStruct((t, 2 * LANES), F32)],
        compiler_params=_cparams(("arbitrary",)),
        name="ssm_project",
    )(x, modt, norm_g3, w_z, w_x, w_dt)

    lane_pad = lambda v: jnp.pad(v, ((0, 0), (0, LANES - v.shape[-1]))).reshape(1, 2 * LANES)
    cw = jnp.pad(conv_w, ((0, SUBLANES - CONV_W), (0, 0)))
    cb = conv_b.reshape(1, CONV_DIM)
    dsk = jnp.repeat(d_skip, SSM_HEADDIM, axis=-1)
    prompt, sample = groups
    scan = functools.partial(_ssd_scan, xbc, dt)
    y1p, y2p, new_state = scan(prompt, cw, cb, lane_pad(dt_bias), lane_pad(a_log), dsk, None, state_out)
    y1s, y2s = scan(sample, cw, cb, lane_pad(dt_bias), lane_pad(a_log), dsk, _pair_state(h0_sample), None)

    to = TOKEN_TILE // 2
    row = lambda c: pl.BlockSpec((to, c), lambda i: (i, 0))
    y_specs, n_p = _two_group_specs(groups, to, D_INNER)
    x = pl.pallas_call(
        functools.partial(_ssm_out_kernel, n_prompt_tiles=n_p),
        grid=(t // to,),
        in_specs=y_specs + y_specs + [row(D_INNER), _const_spec((1, D_INNER)),
                  _const_spec((D_INNER, D_MODEL)), row(D_MODEL), _mod_spec(groups, to, slot)],
        out_specs=row(D_MODEL),
        out_shape=jax.ShapeDtypeStruct((t, D_MODEL), F32),
        compiler_params=_cparams(("arbitrary",)),
        name="ssm_out",
    )(y1p, y1s, y2p, y2s, z, norm_g.reshape(1, D_INNER), w_out, x, modt)
    return x, new_state


def kernel(x_prompt, x_sample, cache_mla, state_ssm, c, c_ctx, mod_w, mod_b, norm_g, ffn_w_in, ffn_w_out,
           mla_w_in, mla_q_norm, mla_kv_norm, mla_wq_b, mla_wkv_b, mla_wo, ssm_w_in, ssm_conv_w, ssm_conv_b,
           ssm_dt_bias, ssm_a_log, ssm_d, ssm_norm_g, ssm_w_out, final_norm_g):
    nbp, lp, d = x_prompt.shape
    nbs, ls, _ = x_sample.shape
    prompt = _Group(0, nbp, lp)
    sample = _Group(nbp * lp, nbs, ls)
    groups = (prompt, sample)
    x = jnp.concatenate([x_prompt.reshape(-1, d), x_sample.reshape(-1, d)], axis=0)

    ncond = 1 + nbs
    sc = jnp.concatenate([c_ctx[None, :], c, jnp.zeros((SUBLANES - ncond, d), F32)], axis=0)
    mod = _modulation(sc, mod_w, mod_b)
    modt = mod[:, :ncond].reshape(DEPTH, ncond, 3, 3, d).transpose(1, 0, 2, 3, 4).reshape(ncond, DEPTH * 3, 3, d)

    norm_g3 = norm_g.reshape(DEPTH * 3, 1, d)
    ffn_in = ffn_w_in.astype(BF16)
    ffn_out = ffn_w_out.astype(BF16)
    n_ssm = ssm_w_in.shape[0]
    ssm_wz = ssm_w_in[:, :, :D_INNER].astype(BF16)
    ssm_wx = ssm_w_in[:, :, D_INNER:D_INNER + CONV_DIM].astype(BF16)
    wdt = ssm_w_in[:, :, D_INNER + CONV_DIM:].reshape(n_ssm, d, 2, SSM_HEADS)
    ssm_wdt = jnp.pad(wdt, ((0, 0), (0, 0), (0, 0), (0, LANES - SSM_HEADS))).reshape(n_ssm, d, 2 * LANES).astype(BF16)
    ssm_wo = ssm_w_out.astype(BF16)

    cos, sin = _rope_tables(groups)
    new_mla = []
    new_state = None
    tiles_p = prompt.rows // TOKEN_TILE
    tiles_s = sample.rows // TOKEN_TILE
    for i in range(DEPTH):
        j = i // 2
        x = _ffn_half(x, modt, groups, i, 0, norm_g3, ffn_in, ffn_out, final_norm_g)
        if i % 2 == 0:
            x, entry = _mla_layer(x, modt, groups, i, norm_g3, cache_mla[:, j], cos, sin,
                                  mla_w_in[j], mla_q_norm[j], mla_kv_norm[j], mla_wq_b[j], mla_wkv_b[j], mla_wo[j])
            new_mla.append(entry)
        else:
            x, new_state = _ssm_layer(x, modt, groups, i, norm_g3, state_ssm[:, j], (new_state, j, n_ssm),
                                      ssm_wz[j], ssm_wx[j], ssm_wdt[j], ssm_conv_w[j], ssm_conv_b[j],
                                      ssm_dt_bias[j], ssm_a_log[j], ssm_d[j], ssm_norm_g[j], ssm_wo[j])
        if i < DEPTH - 1:
            x = _ffn_half(x, modt, groups, i, 1, norm_g3, ffn_in, ffn_out, final_norm_g)
    last = functools.partial(_ffn_half, x, modt, groups, DEPTH - 1, 1, norm_g3, ffn_in, ffn_out, final_norm_g, True)
    y_prompt = last((0, tiles_p)).reshape(nbp, lp, d)
    y_sample = last((tiles_p, tiles_s)).reshape(nbs, ls, d)
    new_state = new_state.reshape(nbp, n_ssm, 2, SSM_HEADS, SSM_HEADDIM, D_STATE)
    return y_prompt, y_sample, jnp.stack(new_mla, axis=1), new_state
```

```python
import functools
import math

import jax
import jax.numpy as jnp
from jax import lax
from jax.experimental import pallas as pl
from jax.experimental.pallas import tpu as pltpu

F32 = jnp.float32
BF16 = jnp.bfloat16

D_MODEL = 1024
DEPTH = 4
N_MOD = 9
D_FF = 2816
GRID_W = 64
MLA_HEADS = 16
Q_LORA = 512
KV_LORA = 256
QK_NOPE = 64
QK_ROPE = 32
V_HEAD = 64
QK_HEAD = QK_NOPE + QK_ROPE
ROPE_BASE = 10000.0
HEAD_PAD = 128
VT_ROWS = 80
D_INNER = 2048
SSM_HEADDIM = 64
SSM_HEADS = 32
SSM_GROUPS = 4
D_STATE = 128
CONV_W = 5
CONV_DIM = D_INNER + 2 * SSM_GROUPS * D_STATE
CHUNK = 128
EPS = 1e-6
LOG2E = math.log2(math.e)

LANES = 128
SUBLANES = 8
TOKEN_TILE = 512
FF_CHUNK = 256
VMEM_LIMIT = 56 * 1024 * 1024

NT = (((1,), (1,)), ((), ()))
TN = (((0,), (0,)), ((), ()))


def _cparams(sem):
    return pltpu.CompilerParams(dimension_semantics=sem, vmem_limit_bytes=VMEM_LIMIT)


def _const_spec(shape, index=None):
    index = tuple(index or ())
    block = (None,) * len(index) + tuple(shape)
    zeros = (0,) * len(shape)
    return pl.BlockSpec(block, lambda *_: index + zeros, pipeline_mode=pl.Buffered(1))


def _rmsnorm(x, g):
    ms = jnp.mean(x * x, axis=-1, keepdims=True)
    return (x * lax.rsqrt(ms + EPS)) * g


def _silu(x):
    h = 0.5 * x
    return h * jnp.tanh(h) + h


def _adaln(x, g, shift, scale):
    return _rmsnorm(x, g) * (1.0 + scale) + shift


class _Group:
    def __init__(self, row0, nb, length):
        self.row0, self.nb, self.length = row0, nb, length

    @property
    def rows(self):
        return self.nb * self.length


def _mod_kernel(sc_ref, w_ref, b_ref, o_ref):
    s = _silu(sc_ref[...])
    o_ref[...] = jnp.dot(s.astype(BF16), w_ref[...].astype(BF16),
                         preferred_element_type=F32) + b_ref[...]


def _modulation(sc, mod_w, mod_b):
    tn = 1024
    n = N_MOD * D_MODEL
    return pl.pallas_call(
        _mod_kernel,
        grid=(DEPTH, n // tn),
        in_specs=[
            pl.BlockSpec((SUBLANES, D_MODEL), lambda i, j: (0, 0)),
            pl.BlockSpec((None, D_MODEL, tn), lambda i, j: (i, 0, j)),
            pl.BlockSpec((None, 1, tn), lambda i, j: (i, 0, j)),
        ],
        out_specs=pl.BlockSpec((None, SUBLANES, tn), lambda i, j: (i, 0, j)),
        out_shape=jax.ShapeDtypeStruct((DEPTH, SUBLANES, n), F32),
        compiler_params=_cparams(("arbitrary", "arbitrary")),
        name="modulation",
    )(sc, mod_w, mod_b.reshape(DEPTH, 1, n))


def _mod_spec(groups, tile, slot, first_tile=0):
    bounds = []
    for g in groups:
        for b in range(g.nb if g.row0 else 1):
            bounds.append((g.row0 + b * g.length) // tile)

    def index_map(i, *_):
        gidx = 0
        for lo in bounds[1:]:
            gidx = gidx + (i + first_tile >= lo).astype(jnp.int32)
        return (gidx, slot, 0, 0)

    return pl.BlockSpec((None, None, 3, D_MODEL), index_map)


def _two_group_specs(groups, tile, width, axis=0):
    prompt, sample = groups
    n_p = prompt.rows // tile
    n_s = sample.rows // tile
    if axis == 0:
        return [pl.BlockSpec((tile, width), lambda i: (jnp.minimum(i, n_p - 1), 0)),
                pl.BlockSpec((tile, width), lambda i: (jnp.clip(i - n_p, 0, n_s - 1), 0))], n_p
    return [pl.BlockSpec((width, tile), lambda i: (0, jnp.minimum(i, n_p - 1))),
            pl.BlockSpec((width, tile), lambda i: (0, jnp.clip(i - n_p, 0, n_s - 1)))], n_p


def _ffn_kernel(x_ref, mod_ref, g_ref, fg_ref, win_hbm, wout_hbm, o_ref,
                win_b, wout_b, a_scr, stage_in, stage_out, sem, *, layer, half, final_norm):
    n_chunks = D_FF // FF_CHUNK

    def chunk_copies(c):
        lo = c * FF_CHUNK
        slot = c % 2
        w_in = win_hbm.at[layer, half]
        return (pltpu.make_async_copy(w_in.at[:, pl.ds(lo, FF_CHUNK)], stage_in.at[slot, 0], sem.at[slot, 0]),
                pltpu.make_async_copy(w_in.at[:, pl.ds(D_FF + lo, FF_CHUNK)], stage_in.at[slot, 1], sem.at[slot, 1]),
                pltpu.make_async_copy(wout_hbm.at[layer, half, pl.ds(lo, FF_CHUNK), :], stage_out.at[slot],
                                      sem.at[slot, 2]))

    x = x_ref[...]
    m = mod_ref[...]
    hb = _adaln(x, g_ref[...], m[0:1], m[1:2]).astype(BF16)

    def compute_chunk(c):
        lo = c * FF_CHUNK
        gate = jnp.dot(hb, win_b[:, lo:lo + FF_CHUNK], preferred_element_type=F32)
        up = jnp.dot(hb, win_b[:, D_FF + lo:D_FF + lo + FF_CHUNK], preferred_element_type=F32)
        a_scr[:, lo:lo + FF_CHUNK] = (_silu(gate) * up).astype(BF16)

    @pl.when(pl.program_id(0) == 0)
    def _first_tile():
        for cp in chunk_copies(0):
            cp.start()
        for c in range(n_chunks):
            if c + 1 < n_chunks:
                for cp in chunk_copies(c + 1):
                    cp.start()
            for cp in chunk_copies(c):
                cp.wait()
            lo = c * FF_CHUNK
            win_b[:, lo:lo + FF_CHUNK] = stage_in[c % 2, 0].astype(BF16)
            win_b[:, D_FF + lo:D_FF + lo + FF_CHUNK] = stage_in[c % 2, 1].astype(BF16)
            wout_b[lo:lo + FF_CHUNK, :] = stage_out[c % 2].astype(BF16)
            compute_chunk(c)

    @pl.when(pl.program_id(0) > 0)
    def _other_tiles():
        for c in range(n_chunks):
            compute_chunk(c)

    y = jnp.dot(a_scr[...], wout_b[...], preferred_element_type=F32)
    out = x + (0.5 * m[2:3]) * y
    if final_norm:
        out = _rmsnorm(out, fg_ref[...])
    o_ref[...] = out


def _ffn_half(x, modt, groups, layer, half, norm_g3, w_in, w_out, final_g, final_norm=False, tiles=None):
    tm = TOKEN_TILE
    first, count = tiles or (0, x.shape[0] // tm)
    slot = 3 * layer + 2 * half
    return pl.pallas_call(
        functools.partial(_ffn_kernel, layer=layer, half=half, final_norm=final_norm),
        grid=(count,),
        in_specs=[
            pl.BlockSpec((tm, D_MODEL), lambda i: (i + first, 0)),
            _mod_spec(groups, tm, slot, first),
            _const_spec((1, D_MODEL), (slot,)),
            _const_spec((1, D_MODEL)),
            pl.BlockSpec(memory_space=pl.ANY),
            pl.BlockSpec(memory_space=pl.ANY),
        ],
        out_specs=pl.BlockSpec((tm, D_MODEL), lambda i: (i, 0)),
        out_shape=jax.ShapeDtypeStruct((count * tm, D_MODEL), F32),
        scratch_shapes=[pltpu.VMEM((D_MODEL, 2 * D_FF), BF16),
                        pltpu.VMEM((D_FF, D_MODEL), BF16),
                        pltpu.VMEM((tm, D_FF), BF16),
                        pltpu.VMEM((2, 2, D_MODEL, FF_CHUNK), F32),
                        pltpu.VMEM((2, FF_CHUNK, D_MODEL), F32),
                        pltpu.SemaphoreType.DMA((2, 3))],
        compiler_params=_cparams(("arbitrary",)),
        name="ffn_half",
    )(x, modt, norm_g3, final_g.reshape(1, D_MODEL), w_in, w_out)


def _expand_kv(ckv_b, krope, wk_ref, wvt_ref, k_ref, vt_ref):
    kn = jnp.dot(ckv_b, wk_ref[...], preferred_element_type=F32)
    for h in range(MLA_HEADS):
        lo = h * HEAD_PAD
        k_ref[:, lo:lo + HEAD_PAD] = (kn[:, lo:lo + HEAD_PAD] + krope).astype(BF16)
    vt_ref[...] = lax.dot_general(wvt_ref[...], ckv_b, NT, preferred_element_type=F32).astype(BF16)
    ones = jnp.ones((VT_ROWS - V_HEAD, ckv_b.shape[0]), BF16)
    for h in range(MLA_HEADS):
        vt_ref[h * VT_ROWS + V_HEAD:(h + 1) * VT_ROWS, :] = ones


def _mla_proj_kernel(x_ref, mod_ref, g_ref, win_ref, qn_ref, kvn_ref, wq1_ref, wq2_ref,
                     wk_ref, wvt_ref, cos_ref, sin_ref,
                     q_ref, k_ref, vt_ref, ckv_ref, kr_ref):
    x = x_ref[...]
    m = mod_ref[...]
    hb = _adaln(x, g_ref[...], m[0:1], m[1:2]).astype(BF16)
    a = jnp.dot(hb, win_ref[...], preferred_element_type=F32)
    qa = _rmsnorm(a[:, :Q_LORA], qn_ref[...]).astype(BF16)
    ckv = _rmsnorm(a[:, Q_LORA:Q_LORA + KV_LORA], kvn_ref[...])
    kr = a[:, Q_LORA + KV_LORA:Q_LORA + KV_LORA + HEAD_PAD]
    kr_rot = a[:, Q_LORA + KV_LORA + HEAD_PAD:]
    cosk = cos_ref[...]
    sink = sin_ref[...]
    lane = lax.broadcasted_iota(jnp.int32, cosk.shape, 1)
    scale = LOG2E / math.sqrt(QK_HEAD)
    cosq = (cosk + (lane < QK_NOPE).astype(F32)) * scale
    sinq = sink * scale
    q1 = jnp.dot(qa, wq1_ref[...], preferred_element_type=F32)
    q2 = jnp.dot(qa, wq2_ref[...], preferred_element_type=F32)
    for h in range(MLA_HEADS):
        lo = h * HEAD_PAD
        q_ref[:, lo:lo + HEAD_PAD] = (q1[:, lo:lo + HEAD_PAD] * cosq
                                      + q2[:, lo:lo + HEAD_PAD] * sinq).astype(BF16)
    ckv_ref[...] = ckv
    kr_ref[...] = kr
    _expand_kv(ckv.astype(BF16), kr * cosk + kr_rot * sink, wk_ref, wvt_ref, k_ref, vt_ref)


def _mla_ctx_kernel(ckv_ref, kr_ref, wk_ref, wvt_ref, k_ref, vt_ref):
    _expand_kv(ckv_ref[...].astype(BF16), kr_ref[...], wk_ref, wvt_ref, k_ref, vt_ref)


def _mla_attn_kernel(*refs, n_parts, heads):
    q_ref = refs[0]
    kv_refs = refs[1:1 + 2 * n_parts]
    o_ref = refs[1 + 2 * n_parts]
    def scores(h):
        qh = q_ref[:, h * HEAD_PAD:(h + 1) * HEAD_PAD]
        return [lax.dot_general(kv_refs[2 * i][:, h * HEAD_PAD:(h + 1) * HEAD_PAD], qh, NT,
                                preferred_element_type=F32) for i in range(n_parts)]

    ss_next = scores(0)
    for h in range(heads):
        ss = ss_next
        if h + 1 < heads:
            ss_next = scores(h + 1)
        mx = ss[0].max(axis=0, keepdims=True)
        for s in ss[1:]:
            mx = jnp.maximum(mx, s.max(axis=0, keepdims=True))
        acc = None
        for i, s in enumerate(ss):
            p = jnp.exp2(s - mx).astype(BF16)
            o = jnp.dot(kv_refs[2 * i + 1][h * VT_ROWS:(h + 1) * VT_ROWS, :], p, preferred_element_type=F32)
            acc = o if acc is None else acc + o
        o_ref[h * V_HEAD:(h + 1) * V_HEAD, :] = (acc[:V_HEAD] / acc[V_HEAD:V_HEAD + 1]).astype(BF16)


def _mla_attention(q, parts, grp, tq, heads):
    nq = grp.length // tq
    qoff = grp.row0 // tq
    in_specs = [pl.BlockSpec((tq, heads * HEAD_PAD), lambda b, p, i: (qoff + b * nq + i, p))]
    args = [q]
    for k, vt, row0, lk in parts:
        koff = row0 // lk
        in_specs.append(pl.BlockSpec((lk, heads * HEAD_PAD), lambda b, p, i, koff=koff: (koff + b, p)))
        in_specs.append(pl.BlockSpec((heads * VT_ROWS, lk), lambda b, p, i, koff=koff: (p, koff + b)))
        args += [k, vt]
    return pl.pallas_call(
        functools.partial(_mla_attn_kernel, n_parts=len(parts), heads=heads),
        grid=(grp.nb, MLA_HEADS // heads, nq),
        in_specs=in_specs,
        out_specs=pl.BlockSpec((heads * V_HEAD, tq), lambda b, p, i: (p, b * nq + i)),
        out_shape=jax.ShapeDtypeStruct((MLA_HEADS * V_HEAD, grp.rows), BF16),
        compiler_params=_cparams(("arbitrary", "arbitrary", "arbitrary")),
        name="mla_attention",
    )(*args)


def _mla_out_kernel(otp_ref, ots_ref, wo_ref, x_ref, mod_ref, o_ref, *, n_prompt_tiles):
    o_t = jnp.where(pl.program_id(0) < n_prompt_tiles, otp_ref[...], ots_ref[...])
    y = lax.dot_general(o_t, wo_ref[...], TN, preferred_element_type=F32)
    o_ref[...] = x_ref[...] + mod_ref[...][2:3] * y


def _rope_tables(groups):
    prompt, sample = groups
    half = QK_ROPE // 2
    freqs = 1.0 / (ROPE_BASE ** (jnp.arange(0, half, 2, dtype=F32) / half))
    pos_i = jnp.arange(sample.length)
    pos = jnp.stack([pos_i // GRID_W, pos_i % GRID_W], axis=-1).astype(F32)
    ang = pos[:, :, None] * freqs
    ang = jnp.broadcast_to(ang[:, :, None, :], (sample.length, 2, 2, half // 2)).reshape(sample.length, QK_ROPE)
    lanes = lambda v: jnp.pad(v, ((0, 0), (QK_NOPE, HEAD_PAD - QK_HEAD)))
    cos = jnp.concatenate([lanes(jnp.ones((prompt.rows, QK_ROPE), F32))] + [lanes(jnp.cos(ang))] * sample.nb, axis=0)
    sin = jnp.concatenate([jnp.zeros((prompt.rows, HEAD_PAD), F32)] + [lanes(jnp.sin(ang))] * sample.nb, axis=0)
    return cos, sin


def _rot_cols(w):
    q = QK_ROPE // 4
    w4 = w.reshape(w.shape[:-1] + (2, 2, q))
    return jnp.stack([-w4[..., 1, :], w4[..., 0, :]], axis=-2).reshape(w.shape)


def _mla_weights(w_in, wq_b, wkv_b):
    d = w_in.shape[0]
    z = lambda *s: jnp.zeros(s, F32)
    k_r = w_in[:, Q_LORA + KV_LORA:]
    pad_tile = lambda c: jnp.concatenate([z(d, QK_NOPE), c, z(d, HEAD_PAD - QK_HEAD)], axis=1)
    w_in_ext = jnp.concatenate([w_in[:, :Q_LORA + KV_LORA], pad_tile(k_r), pad_tile(_rot_cols(k_r))], axis=1)
    wq = wq_b.reshape(Q_LORA, MLA_HEADS, QK_HEAD)
    zq = lambda n: z(Q_LORA, MLA_HEADS, n)
    wq1 = jnp.concatenate([wq, zq(HEAD_PAD - QK_HEAD)], axis=-1)
    wq2 = jnp.concatenate([zq(QK_NOPE), _rot_cols(wq[..., QK_NOPE:]), zq(HEAD_PAD - QK_HEAD)], axis=-1)
    wkv = wkv_b.reshape(KV_LORA, MLA_HEADS, QK_NOPE + V_HEAD)
    wk = jnp.concatenate([wkv[..., :QK_NOPE], z(KV_LORA, MLA_HEADS, HEAD_PAD - QK_NOPE)], axis=-1)
    wv = jnp.concatenate([wkv[..., QK_NOPE:], z(KV_LORA, MLA_HEADS, VT_ROWS - V_HEAD)], axis=-1)
    wvt = wv.reshape(KV_LORA, MLA_HEADS * VT_ROWS).T
    flat = lambda w: w.reshape(w.shape[0], -1).astype(BF16)
    return w_in_ext.astype(BF16), flat(wq1), flat(wq2), flat(wk), wvt.astype(BF16)


def _mla_layer(x, modt, groups, layer, norm_g3, cache, cos, sin, w_in, q_norm, kv_norm, wq_b, wkv_b, wo):
    t = x.shape[0]
    tm = TOKEN_TILE
    slot = 3 * layer + 1
    w_in_ext, wq1, wq2, wk, wvt = _mla_weights(w_in, wq_b, wkv_b)
    hd = MLA_HEADS * HEAD_PAD
    vr = MLA_HEADS * VT_ROWS
    row = lambda c: pl.BlockSpec((tm, c), lambda i: (i, 0))
    q, k, vt, ckv, kr = pl.pallas_call(
        _mla_proj_kernel,
        grid=(t // tm,),
        in_specs=[
            row(D_MODEL), _mod_spec(groups, tm, slot), _const_spec((1, D_MODEL), (slot,)),
            _const_spec(w_in_ext.shape), _const_spec((1, Q_LORA)), _const_spec((1, KV_LORA)),
            _const_spec(wq1.shape), _const_spec(wq2.shape), _const_spec(wk.shape), _const_spec(wvt.shape),
            row(HEAD_PAD), row(HEAD_PAD),
        ],
        out_specs=[row(hd), row(hd), pl.BlockSpec((vr, tm), lambda i: (0, i)), row(KV_LORA), row(HEAD_PAD)],
        out_shape=[jax.ShapeDtypeStruct((t, hd), BF16), jax.ShapeDtypeStruct((t, hd), BF16),
                   jax.ShapeDtypeStruct((vr, t), BF16), jax.ShapeDtypeStruct((t, KV_LORA), F32),
                   jax.ShapeDtypeStruct((t, HEAD_PAD), F32)],
        compiler_params=_cparams(("arbitrary",)),
        name="mla_project",
    )(x, modt, norm_g3, w_in_ext, q_norm.reshape(1, -1), kv_norm.reshape(1, -1),
      wq1, wq2, wk, wvt, cos, sin)

    prompt, sample = groups
    nb, past, _ = cache.shape
    cflat = cache.reshape(nb * past, -1)
    ckv_c = cflat[:, :KV_LORA]
    kr_c = jnp.pad(cflat[:, KV_LORA:], ((0, 0), (QK_NOPE, HEAD_PAD - QK_HEAD)))
    crow = lambda c: pl.BlockSpec((past, c), lambda i: (i, 0))
    k_c, vt_c = pl.pallas_call(
        _mla_ctx_kernel,
        grid=(nb,),
        in_specs=[crow(KV_LORA), crow(HEAD_PAD), _const_spec(wk.shape), _const_spec(wvt.shape)],
        out_specs=[crow(hd), pl.BlockSpec((vr, past), lambda i: (0, i))],
        out_shape=[jax.ShapeDtypeStruct((nb * past, hd), BF16), jax.ShapeDtypeStruct((vr, nb * past), BF16)],
        compiler_params=_cparams(("arbitrary",)),
        name="mla_ctx_expand",
    )(ckv_c, kr_c, wk, wvt)

    ot_p = _mla_attention(q, [(k, vt, prompt.row0, prompt.length)], prompt, 256, MLA_HEADS)
    ot_s = _mla_attention(q, [(k_c, vt_c, 0, past), (k, vt, sample.row0, sample.length)], sample, 512, 4)
    vd = MLA_HEADS * V_HEAD
    o_specs, n_p = _two_group_specs(groups, tm, vd, axis=1)
    x = pl.pallas_call(
        functools.partial(_mla_out_kernel, n_prompt_tiles=n_p),
        grid=(t // tm,),
        in_specs=o_specs + [_const_spec((vd, D_MODEL)), row(D_MODEL), _mod_spec(groups, tm, slot)],
        out_specs=row(D_MODEL),
        out_shape=jax.ShapeDtypeStruct((t, D_MODEL), F32),
        compiler_params=_cparams(("arbitrary",)),
        name="mla_out",
    )(ot_p, ot_s, wo.astype(BF16), x, modt)
    entry = jnp.concatenate([ckv[:prompt.rows], kr[:prompt.rows, QK_NOPE:QK_HEAD]], axis=-1)
    return x, entry.reshape(prompt.nb, prompt.length, -1)


def _ssm_proj_kernel(x_ref, mod_ref, g_ref, wz_ref, wx_ref, wdt_ref, z_ref, xbc_ref, dt_ref):
    m = mod_ref[...]
    hb = _adaln(x_ref[...], g_ref[...], m[0:1], m[1:2]).astype(BF16)
    z_ref[...] = jnp.dot(hb, wz_ref[...], preferred_element_type=F32).astype(BF16)
    xbc_ref[...] = jnp.dot(hb, wx_ref[...], preferred_element_type=F32)
    dt_ref[...] = jnp.dot(hb, wdt_ref[...], preferred_element_type=F32)


def _split3(a):
    hi = a.astype(BF16)
    r1 = a - hi.astype(F32)
    mid = r1.astype(BF16)
    lo = (r1 - mid.astype(F32)).astype(BF16)
    return hi, mid, lo


def _ssd_kernel(*refs, zero_init, emit_state, layer, nc):
    it = iter(refs)
    cur_ref, prev_ref, next_ref, dt_ref = next(it), next(it), next(it), next(it)
    cw_ref, cb_ref, dtb_ref, alog_ref, dskip_ref = next(it), next(it), next(it), next(it), next(it)
    h0_ref = None if zero_init else next(it)
    prev_st_ref = next(it) if layer else None
    y1_ref, y2_ref = next(it), next(it)
    st_ref = next(it) if emit_state else None
    s_f, s_b, sb_loc, c_keep, ecb_keep, etb_keep, xpad, act = (next(it) for _ in range(8))

    q = CHUNK
    step = pl.program_id(1)
    npairs = SSM_HEADS // 2
    hpg = SSM_HEADS // SSM_GROUPS
    lane = lax.broadcasted_iota(jnp.int32, (q, LANES), 1)
    left = lane < SSM_HEADDIM

    def pair_cols(v, p):
        return jnp.where(left[:v.shape[0]], v[:, 2 * p:2 * p + 1], v[:, 2 * p + 1:2 * p + 2])

    @pl.when(step == 0)
    def _init():
        if zero_init:
            s_f[...] = jnp.zeros(s_f.shape, F32)
            s_b[...] = jnp.zeros(s_b.shape, F32)
        else:
            s_f[...] = h0_ref[0]
            s_b[...] = h0_ref[1]

    @pl.when(step < nc)
    def _sweep_up():
        c = step
        blk = 4 * SUBLANES
        st = blk // SUBLANES
        has_prev = c > 0
        has_next = c < nc - 1
        for j in range(CONV_DIM // LANES):
            cols = slice(j * LANES, (j + 1) * LANES)
            xpad[j, 0:SUBLANES, :] = jnp.where(has_prev, prev_ref[:, cols], 0.0)
            xpad[j, SUBLANES:SUBLANES + q, :] = cur_ref[:, cols]
            xpad[j, SUBLANES + q:, :] = jnp.where(has_next, next_ref[:, cols], 0.0)
            wk = [jnp.broadcast_to(cw_ref[k:k + 1, cols], (SUBLANES, LANES)) for k in range(CONV_W)]
            bias = jnp.broadcast_to(cb_ref[:, cols], (SUBLANES, LANES))
            for b in range(q // blk):
                r0 = SUBLANES + b * blk - CONV_W // 2
                taps = [xpad[j, pl.ds(r0 + u, SUBLANES, stride=st), :] for u in range(st + CONV_W - 1)]
                for v in range(st):
                    acc = bias
                    for k in range(CONV_W):
                        acc = acc + taps[v + k] * wk[k]
                    act[j, pl.ds(b * blk + v, SUBLANES, stride=st), :] = _silu(acc)
        n_x = D_INNER // LANES
        for g in range(SSM_GROUPS):
            c_keep[c, :, g * D_STATE:(g + 1) * D_STATE] = act[n_x + SSM_GROUPS + g].astype(BF16)

        dtr = dt_ref[...] + dtb_ref[...]
        dtv = jnp.maximum(dtr, 0.0) + jnp.log1p(jnp.exp(-jnp.abs(dtr)))
        a2 = dtv * (-LOG2E * jnp.exp(alog_ref[...]))
        ri = lax.broadcasted_iota(jnp.int32, (q, q), 0)
        ci = lax.broadcasted_iota(jnp.int32, (q, q), 1)
        lower = ci <= ri
        slower = ci < ri
        supper = ci > ri
        tot = jnp.sum(a2, axis=0, keepdims=True)
        dirs = []
        for d, keep in enumerate((lower, ci >= ri)):
            ad = a2[:, d * LANES:(d + 1) * LANES]
            dtd = dtv[:, d * LANES:(d + 1) * LANES]
            td = tot[:, d * LANES:(d + 1) * LANES]
            tri = keep.astype(BF16)
            cum = sum(jnp.dot(tri, part, preferred_element_type=F32) for part in _split3(ad))
            dirs.append(dict(cum=cum, cum_t=cum.T, dt_t=dtd.T, ecum=jnp.exp2(cum), etot=jnp.exp2(td),
                             wout_t=(jnp.exp2(td - cum) * dtd).T))
        fw, bw = dirs
        dt_sum_t = fw["dt_t"] + bw["dt_t"]
        ecb_keep[c] = bw["ecum"]
        etb_keep[c] = jnp.broadcast_to(bw["etot"], (SUBLANES, LANES))
        dsum = dskip_ref[0:1, :] + dskip_ref[1:2, :]

        for g in range(SSM_GROUPS):
            bm = act[n_x + g]
            cm_b = c_keep[c, :, g * D_STATE:(g + 1) * D_STATE]
            bm_t = bm.T
            cb = lax.dot_general(cm_b, bm.astype(BF16), NT, preferred_element_type=F32)
            for pp in range(hpg // 2):
                p = g * (hpg // 2) + pp
                xp = act[p]
                xbd = jnp.concatenate([jnp.where(left, xp, 0.0), jnp.where(left, 0.0, xp)], axis=0).astype(BF16)
                ms, bfs, bbs = [], [], []
                for h in (2 * p, 2 * p + 1):
                    seg = jnp.where(lower, fw["cum"][:, h:h + 1] - fw["cum_t"][h:h + 1, :],
                                    bw["cum"][:, h:h + 1] - bw["cum_t"][h:h + 1, :])
                    wgt = jnp.where(slower, fw["dt_t"][h:h + 1, :],
                                    jnp.where(supper, bw["dt_t"][h:h + 1, :], dt_sum_t[h:h + 1, :]))
                    ms.append((cb * jnp.exp2(seg) * wgt).astype(BF16))
                    bfs.append((bm_t * fw["wout_t"][h:h + 1, :]).astype(BF16))
                    bbs.append((bm_t * bw["wout_t"][h:h + 1, :]).astype(BF16))
                y = jnp.dot(jnp.concatenate(ms, axis=1), xbd, preferred_element_type=F32)
                st = s_f[p]
                y_off = jnp.dot(cm_b, st.astype(BF16), preferred_element_type=F32)
                y1_ref[:, p * LANES:(p + 1) * LANES] = (y + y_off * pair_cols(fw["ecum"], p)
                                                        + dsum[:, p * LANES:(p + 1) * LANES] * xp)
                s_f[p] = st * pair_cols(fw["etot"], p) + jnp.dot(jnp.concatenate(bfs, axis=1), xbd,
                                                                 preferred_element_type=F32)
                sb_loc[c, p] = jnp.dot(jnp.concatenate(bbs, axis=1), xbd, preferred_element_type=F32)

    @pl.when(step >= nc)
    def _sweep_down():
        c = 2 * nc - 1 - step
        ecb = ecb_keep[c]
        etb = etb_keep[c][0:1]
        for p in range(npairs):
            g = p // (hpg // 2)
            st = s_b[p]
            y_off = jnp.dot(c_keep[c, :, g * D_STATE:(g + 1) * D_STATE], st.astype(BF16),
                            preferred_element_type=F32)
            y2_ref[:, p * LANES:(p + 1) * LANES] = y_off * pair_cols(ecb, p)
            s_b[p] = st * pair_cols(etb, p) + sb_loc[c, p]

        if emit_state:
            @pl.when(step == 2 * nc - 1)
            def _emit():
                for l in range(layer):
                    st_ref[l] = prev_st_ref[l]
                for p in range(npairs):
                    st_ref[layer, 0, p] = s_f[p].T
                    st_ref[layer, 1, p] = s_b[p].T


def _ssd_scan(xbc, dt, grp, conv_w, conv_b, dt_bias, a_log, d_skip, h0, state_out):
    t = xbc.shape[0]
    q = CHUNK
    nc = grp.length // q
    zero_init = h0 is None
    emit_state = state_out is not None
    boff = grp.row0 // q
    npairs = SSM_HEADS // 2
    qs = q // SUBLANES

    chunk = lambda b, s: boff + b * nc + jnp.minimum(s, nc - 1)
    in_specs = [
        pl.BlockSpec((q, CONV_DIM), lambda b, s: (chunk(b, s), 0)),
        pl.BlockSpec((SUBLANES, CONV_DIM), lambda b, s: (jnp.maximum(chunk(b, s) * qs - 1, 0), 0)),
        pl.BlockSpec((SUBLANES, CONV_DIM), lambda b, s: (jnp.minimum((chunk(b, s) + 1) * qs, t // SUBLANES - 1), 0)),
        pl.BlockSpec((q, 2 * LANES), lambda b, s: (chunk(b, s), 0)),
        _const_spec((SUBLANES, CONV_DIM)), _const_spec((1, CONV_DIM)),
        _const_spec((1, 2 * LANES)), _const_spec((1, 2 * LANES)), _const_spec((2, D_INNER)),
    ]
    args = [xbc, xbc, xbc, dt, conv_w, conv_b, dt_bias, a_log, d_skip]
    if not zero_init:
        in_specs.append(pl.BlockSpec((None, 2, npairs, D_STATE, LANES), lambda b, s: (b, 0, 0, 0, 0)))
        args.append(h0)
    state_block = lambda n: pl.BlockSpec((None, n, 2, npairs, LANES, D_STATE), lambda b, s: (b, 0, 0, 0, 0, 0))
    layer = state_out[1] if emit_state else 0
    if layer:
        in_specs.append(state_block(layer))
        args.append(state_out[0])
    out_specs = [
        pl.BlockSpec((q, D_INNER), lambda b, s: (b * nc + jnp.minimum(s, nc - 1), 0)),
        pl.BlockSpec((q, D_INNER), lambda b, s: (b * nc + jnp.where(s < nc, nc - 1, 2 * nc - 1 - s), 0)),
    ]
    out_shape = [jax.ShapeDtypeStruct((grp.rows, D_INNER), F32)] * 2
    if emit_state:
        out_specs.append(state_block(layer + 1))
        out_shape.append(jax.ShapeDtypeStruct((grp.nb, layer + 1, 2, npairs, LANES, D_STATE), F32))
    return pl.pallas_call(
        functools.partial(_ssd_kernel, zero_init=zero_init, emit_state=emit_state, layer=layer, nc=nc),
        grid=(grp.nb, 2 * nc),
        in_specs=in_specs,
        out_specs=out_specs,
        out_shape=out_shape,
        scratch_shapes=[pltpu.VMEM((npairs, D_STATE, LANES), F32),
                        pltpu.VMEM((npairs, D_STATE, LANES), F32),
                        pltpu.VMEM((nc, npairs, D_STATE, LANES), F32),
                        pltpu.VMEM((nc, q, SSM_GROUPS * D_STATE), BF16),
                        pltpu.VMEM((nc, q, LANES), F32),
                        pltpu.VMEM((nc, SUBLANES, LANES), F32),
                        pltpu.VMEM((CONV_DIM // LANES, q + 2 * SUBLANES, LANES), F32),
                        pltpu.VMEM((CONV_DIM // LANES, q, LANES), F32)],
        compiler_params=_cparams(("arbitrary", "arbitrary")),
        name="ssd_scan",
    )(*args)


def _ssm_out_kernel(y1p_ref, y1s_ref, y2p_ref, y2s_ref, z_ref, ng_ref, w_ref, x_ref, mod_ref, o_ref,
                    *, n_prompt_tiles):
    is_prompt = pl.program_id(0) < n_prompt_tiles
    y = jnp.where(is_prompt, y1p_ref[...] + y2p_ref[...], y1s_ref[...] + y2s_ref[...])
    y = y * _silu(z_ref[...].astype(F32))
    yn = _rmsnorm(y, ng_ref[...]).astype(BF16)
    o_ref[...] = x_ref[...] + mod_ref[...][2:3] * jnp.dot(yn, w_ref[...], preferred_element_type=F32)


def _pair_state(h):
    b = h.shape[0]
    return h.reshape(b, 2, SSM_HEADS // 2, 2, SSM_HEADDIM, D_STATE).transpose(0, 1, 2, 5, 3, 4).reshape(
        b, 2, SSM_HEADS // 2, D_STATE, 2 * SSM_HEADDIM)


def _ssm_layer(x, modt, groups, layer, norm_g3, h0_sample, state_out, w_z, w_x, w_dt, conv_w, conv_b,
               dt_bias, a_log, d_skip, norm_g, w_out):
    t = x.shape[0]
    tm = TOKEN_TILE
    slot = 3 * layer + 1
    row = lambda c: pl.BlockSpec((tm, c), lambda i: (i, 0))
    z, xbc, dt = pl.pallas_call(
        _ssm_proj_kernel,
        grid=(t // tm,),
        in_specs=[row(D_MODEL), _mod_spec(groups, tm, slot), _const_spec((1, D_MODEL), (slot,)),
                  _const_spec(w_z.shape), _const_spec(w_x.shape), _const_spec(w_dt.shape)],
        out_specs=[row(D_INNER), row(CONV_DIM), row(2 * LANES)],
        out_shape=[jax.ShapeDtypeStruct((t, D_INNER), BF16), jax.ShapeDtypeStruct((t, CONV_DIM), F32),
                   jax.ShapeDtypeStruct((t, 2 * LANES), F32)],
        compiler_params=_cparams(("arbitrary",)),
        name="ssm_project",
    )(x, modt, norm_g3, w_z, w_x, w_dt)

    lane_pad = lambda v: jnp.pad(v, ((0, 0), (0, LANES - v.shape[-1]))).reshape(1, 2 * LANES)
    cw = jnp.pad(conv_w, ((0, SUBLANES - CONV_W), (0, 0)))
    cb = conv_b.reshape(1, CONV_DIM)
    dsk = jnp.repeat(d_skip, SSM_HEADDIM, axis=-1)
    prompt, sample = groups
    scan = functools.partial(_ssd_scan, xbc, dt)
    y1p, y2p, new_state = scan(prompt, cw, cb, lane_pad(dt_bias), lane_pad(a_log), dsk, None, state_out)
    y1s, y2s = scan(sample, cw, cb, lane_pad(dt_bias), lane_pad(a_log), dsk, _pair_state(h0_sample), None)

    to = TOKEN_TILE // 2
    row = lambda c: pl.BlockSpec((to, c), lambda i: (i, 0))
    y_specs, n_p = _two_group_specs(groups, to, D_INNER)
    x = pl.pallas_call(
        functools.partial(_ssm_out_kernel, n_prompt_tiles=n_p),
        grid=(t // to,),
        in_specs=y_specs + y_specs + [row(D_INNER), _const_spec((1, D_INNER)),
                  _const_spec((D_INNER, D_MODEL)), row(D_MODEL), _mod_spec(groups, to, slot)],
        out_specs=row(D_MODEL),
        out_shape=jax.ShapeDtypeStruct((t, D_MODEL), F32),
        compiler_params=_cparams(("arbitrary",)),
        name="ssm_out",
    )(y1p, y1s, y2p, y2s, z, norm_g.reshape(1, D_INNER), w_out, x, modt)
    return x, new_state


def kernel(x_prompt, x_sample, cache_mla, state_ssm, c, c_ctx, mod_w, mod_b, norm_g, ffn_w_in, ffn_w_out,
           mla_w_in, mla_q_norm, mla_kv_norm, mla_wq_b, mla_wkv_b, mla_wo, ssm_w_in, ssm_conv_w, ssm_conv_b,
           ssm_dt_bias, ssm_a_log, ssm_d, ssm_norm_g, ssm_w_out, final_norm_g):
    nbp, lp, d = x_prompt.shape
    nbs, ls, _ = x_sample.shape
    prompt = _Group(0, nbp, lp)
    sample = _Group(nbp * lp, nbs, ls)
    groups = (prompt, sample)
    x = jnp.concatenate([x_prompt.reshape(-1, d), x_sample.reshape(-1, d)], axis=0)

    ncond = 1 + nbs
    sc = jnp.concatenate([c_ctx[None, :], c, jnp.zeros((SUBLANES - ncond, d), F32)], axis=0)
    mod = _modulation(sc, mod_w, mod_b)
    modt = mod[:, :ncond].reshape(DEPTH, ncond, 3, 3, d).transpose(1, 0, 2, 3, 4).reshape(ncond, DEPTH * 3, 3, d)

    norm_g3 = norm_g.reshape(DEPTH * 3, 1, d)
    ffn_in = ffn_w_in
    ffn_out = ffn_w_out
    n_ssm = ssm_w_in.shape[0]
    ssm_wz = ssm_w_in[:, :, :D_INNER].astype(BF16)
    ssm_wx = ssm_w_in[:, :, D_INNER:D_INNER + CONV_DIM].astype(BF16)
    wdt = ssm_w_in[:, :, D_INNER + CONV_DIM:].reshape(n_ssm, d, 2, SSM_HEADS)
    ssm_wdt = jnp.pad(wdt, ((0, 0), (0, 0), (0, 0), (0, LANES - SSM_HEADS))).reshape(n_ssm, d, 2 * LANES).astype(BF16)
    ssm_wo = ssm_w_out.astype(BF16)

    cos, sin = _rope_tables(groups)
    new_mla = []
    new_state = None
    tiles_p = prompt.rows // TOKEN_TILE
    tiles_s = sample.rows // TOKEN_TILE
    for i in range(DEPTH):
        j = i // 2
        x = _ffn_half(x, modt, groups, i, 0, norm_g3, ffn_in, ffn_out, final_norm_g)
        if i % 2 == 0:
            x, entry = _mla_layer(x, modt, groups, i, norm_g3, cache_mla[:, j], cos, sin,
                                  mla_w_in[j], mla_q_norm[j], mla_kv_norm[j], mla_wq_b[j], mla_wkv_b[j], mla_wo[j])
            new_mla.append(entry)
        else:
            x, new_state = _ssm_layer(x, modt, groups, i, norm_g3, state_ssm[:, j], (new_state, j),
                                      ssm_wz[j], ssm_wx[j], ssm_wdt[j], ssm_conv_w[j], ssm_conv_b[j],
                                      ssm_dt_bias[j], ssm_a_log[j], ssm_d[j], ssm_norm_g[j], ssm_wo[j])
        if i < DEPTH - 1:
            x = _ffn_half(x, modt, groups, i, 1, norm_g3, ffn_in, ffn_out, final_norm_g)
    last = functools.partial(_ffn_half, x, modt, groups, DEPTH - 1, 1, norm_g3, ffn_in, ffn_out, final_norm_g, True)
    y_prompt = last((0, tiles_p)).reshape(nbp, lp, d)
    y_sample = last((tiles_p, tiles_s)).reshape(nbs, ls, d)
    new_state = new_state.reshape(nbp, n_ssm, 2, SSM_HEADS, SSM_HEADDIM, D_STATE)
    return y_prompt, y_sample, jnp.stack(new_mla, axis=1), new_state
```

```python
import functools
import math

import jax
import jax.numpy as jnp
from jax import lax
from jax.experimental import pallas as pl
from jax.experimental.pallas import tpu as pltpu

F32 = jnp.float32
BF16 = jnp.bfloat16

D_MODEL = 1024
DEPTH = 4
N_MOD = 9
D_FF = 2816
GRID_W = 64
MLA_HEADS = 16
Q_LORA = 512
KV_LORA = 256
QK_NOPE = 64
QK_ROPE = 32
V_HEAD = 64
QK_HEAD = QK_NOPE + QK_ROPE
ROPE_BASE = 10000.0
HEAD_PAD = 128
VT_ROWS = 80
D_INNER = 2048
SSM_HEADDIM = 64
SSM_HEADS = 32
SSM_GROUPS = 4
D_STATE = 128
CONV_W = 5
CONV_DIM = D_INNER + 2 * SSM_GROUPS * D_STATE
CHUNK = 128
EPS = 1e-6
LOG2E = math.log2(math.e)

LANES = 128
SUBLANES = 8
TOKEN_TILE = 512
FF_CHUNK = 256
FFN_STAGE_SLOTS = 4
VMEM_LIMIT = 56 * 1024 * 1024

NT = (((1,), (1,)), ((), ()))
TN = (((0,), (0,)), ((), ()))


def _cparams(sem):
    return pltpu.CompilerParams(dimension_semantics=sem, vmem_limit_bytes=VMEM_LIMIT)


def _const_spec(shape, index=None):
    index = tuple(index or ())
    block = (None,) * len(index) + tuple(shape)
    zeros = (0,) * len(shape)
    return pl.BlockSpec(block, lambda *_: index + zeros, pipeline_mode=pl.Buffered(1))


def _rmsnorm(x, g):
    ms = jnp.mean(x * x, axis=-1, keepdims=True)
    return (x * lax.rsqrt(ms + EPS)) * g


def _silu(x):
    h = 0.5 * x
    return h * jnp.tanh(h) + h


def _adaln(x, g, shift, scale):
    return _rmsnorm(x, g) * (1.0 + scale) + shift


class _Group:
    def __init__(self, row0, nb, length):
        self.row0, self.nb, self.length = row0, nb, length

    @property
    def rows(self):
        return self.nb * self.length


def _mod_kernel(sc_ref, w_ref, b_ref, o_ref):
    s = _silu(sc_ref[...])
    o_ref[...] = jnp.dot(s.astype(BF16), w_ref[...].astype(BF16),
                         preferred_element_type=F32) + b_ref[...]


def _modulation(sc, mod_w, mod_b):
    tn = 1024
    n = N_MOD * D_MODEL
    return pl.pallas_call(
        _mod_kernel,
        grid=(DEPTH, n // tn),
        in_specs=[
            pl.BlockSpec((SUBLANES, D_MODEL), lambda i, j: (0, 0)),
            pl.BlockSpec((None, D_MODEL, tn), lambda i, j: (i, 0, j)),
            pl.BlockSpec((None, 1, tn), lambda i, j: (i, 0, j)),
        ],
        out_specs=pl.BlockSpec((None, SUBLANES, tn), lambda i, j: (i, 0, j)),
        out_shape=jax.ShapeDtypeStruct((DEPTH, SUBLANES, n), F32),
        compiler_params=_cparams(("arbitrary", "arbitrary")),
        name="modulation",
    )(sc, mod_w, mod_b.reshape(DEPTH, 1, n))


def _mod_spec(groups, tile, slot, first_tile=0):
    bounds = []
    for g in groups:
        for b in range(g.nb if g.row0 else 1):
            bounds.append((g.row0 + b * g.length) // tile)

    def index_map(i, *_):
        gidx = 0
        for lo in bounds[1:]:
            gidx = gidx + (i + first_tile >= lo).astype(jnp.int32)
        return (gidx, slot, 0, 0)

    return pl.BlockSpec((None, None, 3, D_MODEL), index_map)


def _two_group_specs(groups, tile, width, axis=0):
    prompt, sample = groups
    n_p = prompt.rows // tile
    n_s = sample.rows // tile
    if axis == 0:
        return [pl.BlockSpec((tile, width), lambda i: (jnp.minimum(i, n_p - 1), 0)),
                pl.BlockSpec((tile, width), lambda i: (jnp.clip(i - n_p, 0, n_s - 1), 0))], n_p
    return [pl.BlockSpec((width, tile), lambda i: (0, jnp.minimum(i, n_p - 1))),
            pl.BlockSpec((width, tile), lambda i: (0, jnp.clip(i - n_p, 0, n_s - 1)))], n_p


def _ffn_kernel(x_ref, mod_ref, g_ref, fg_ref, win_hbm, wout_hbm, o_ref,
                win_b, wout_b, a_scr, stage_in, stage_out, sem, *, layer, half, final_norm):
    n_chunks = D_FF // FF_CHUNK
    n_slots = stage_out.shape[0]
    ahead = n_slots - 1

    def chunk_copies(c):
        lo = c * FF_CHUNK
        slot = c % n_slots
        w_in = win_hbm.at[layer, half]
        return (pltpu.make_async_copy(w_in.at[:, pl.ds(lo, FF_CHUNK)], stage_in.at[slot, 0], sem.at[slot, 0]),
                pltpu.make_async_copy(w_in.at[:, pl.ds(D_FF + lo, FF_CHUNK)], stage_in.at[slot, 1], sem.at[slot, 1]),
                pltpu.make_async_copy(wout_hbm.at[layer, half, pl.ds(lo, FF_CHUNK), :], stage_out.at[slot],
                                      sem.at[slot, 2]))

    x = x_ref[...]
    m = mod_ref[...]
    hb = _adaln(x, g_ref[...], m[0:1], m[1:2]).astype(BF16)

    def compute_chunk(c):
        lo = c * FF_CHUNK
        gate = jnp.dot(hb, win_b[:, lo:lo + FF_CHUNK], preferred_element_type=F32)
        up = jnp.dot(hb, win_b[:, D_FF + lo:D_FF + lo + FF_CHUNK], preferred_element_type=F32)
        a_scr[:, lo:lo + FF_CHUNK] = (_silu(gate) * up).astype(BF16)

    @pl.when(pl.program_id(0) == 0)
    def _first_tile():
        for c in range(ahead):
            for cp in chunk_copies(c):
                cp.start()
        for c in range(n_chunks):
            if c + ahead < n_chunks:
                for cp in chunk_copies(c + ahead):
                    cp.start()
            for cp in chunk_copies(c):
                cp.wait()
            lo = c * FF_CHUNK
            slot = c % n_slots
            win_b[:, lo:lo + FF_CHUNK] = stage_in[slot, 0].astype(BF16)
            win_b[:, D_FF + lo:D_FF + lo + FF_CHUNK] = stage_in[slot, 1].astype(BF16)
            wout_b[lo:lo + FF_CHUNK, :] = stage_out[slot].astype(BF16)
            compute_chunk(c)

    @pl.when(pl.program_id(0) > 0)
    def _other_tiles():
        for c in range(n_chunks):
            compute_chunk(c)

    y = jnp.dot(a_scr[...], wout_b[...], preferred_element_type=F32)
    out = x + (0.5 * m[2:3]) * y
    if final_norm:
        out = _rmsnorm(out, fg_ref[...])
    o_ref[...] = out


def _ffn_half(x, modt, groups, layer, half, norm_g3, w_in, w_out, final_g, final_norm=False, tiles=None):
    tm = TOKEN_TILE
    first, count = tiles or (0, x.shape[0] // tm)
    slot = 3 * layer + 2 * half
    return pl.pallas_call(
        functools.partial(_ffn_kernel, layer=layer, half=half, final_norm=final_norm),
        grid=(count,),
        in_specs=[
            pl.BlockSpec((tm, D_MODEL), lambda i: (i + first, 0)),
            _mod_spec(groups, tm, slot, first),
            _const_spec((1, D_MODEL), (slot,)),
            _const_spec((1, D_MODEL)),
            pl.BlockSpec(memory_space=pl.ANY),
            pl.BlockSpec(memory_space=pl.ANY),
        ],
        out_specs=pl.BlockSpec((tm, D_MODEL), lambda i: (i, 0)),
        out_shape=jax.ShapeDtypeStruct((count * tm, D_MODEL), F32),
        scratch_shapes=[pltpu.VMEM((D_MODEL, 2 * D_FF), BF16),
                        pltpu.VMEM((D_FF, D_MODEL), BF16),
                        pltpu.VMEM((tm, D_FF), BF16),
                        pltpu.VMEM((FFN_STAGE_SLOTS, 2, D_MODEL, FF_CHUNK), F32),
                        pltpu.VMEM((FFN_STAGE_SLOTS, FF_CHUNK, D_MODEL), F32),
                        pltpu.SemaphoreType.DMA((FFN_STAGE_SLOTS, 3))],
        compiler_params=_cparams(("arbitrary",)),
        name="ffn_half",
    )(x, modt, norm_g3, final_g.reshape(1, D_MODEL), w_in, w_out)


def _expand_kv(ckv_b, krope, wk_ref, wvt_ref, k_ref, vt_ref):
    kn = jnp.dot(ckv_b, wk_ref[...], preferred_element_type=F32)
    for h in range(MLA_HEADS):
        lo = h * HEAD_PAD
        k_ref[:, lo:lo + HEAD_PAD] = (kn[:, lo:lo + HEAD_PAD] + krope).astype(BF16)
    vt_ref[...] = lax.dot_general(wvt_ref[...], ckv_b, NT, preferred_element_type=F32).astype(BF16)
    ones = jnp.ones((VT_ROWS - V_HEAD, ckv_b.shape[0]), BF16)
    for h in range(MLA_HEADS):
        vt_ref[h * VT_ROWS + V_HEAD:(h + 1) * VT_ROWS, :] = ones


def _mla_proj_kernel(x_ref, mod_ref, g_ref, win_ref, qn_ref, kvn_ref, wq1_ref, wq2_ref,
                     wk_ref, wvt_ref, cos_ref, sin_ref,
                     q_ref, k_ref, vt_ref, ckv_ref, kr_ref):
    x = x_ref[...]
    m = mod_ref[...]
    hb = _adaln(x, g_ref[...], m[0:1], m[1:2]).astype(BF16)
    a = jnp.dot(hb, win_ref[...], preferred_element_type=F32)
    qa = _rmsnorm(a[:, :Q_LORA], qn_ref[...]).astype(BF16)
    ckv = _rmsnorm(a[:, Q_LORA:Q_LORA + KV_LORA], kvn_ref[...])
    kr = a[:, Q_LORA + KV_LORA:Q_LORA + KV_LORA + HEAD_PAD]
    kr_rot = a[:, Q_LORA + KV_LORA + HEAD_PAD:]
    cosk = cos_ref[...]
    sink = sin_ref[...]
    lane = lax.broadcasted_iota(jnp.int32, cosk.shape, 1)
    scale = LOG2E / math.sqrt(QK_HEAD)
    cosq = (cosk + (lane < QK_NOPE).astype(F32)) * scale
    sinq = sink * scale
    q1 = jnp.dot(qa, wq1_ref[...], preferred_element_type=F32)
    q2 = jnp.dot(qa, wq2_ref[...], preferred_element_type=F32)
    for h in range(MLA_HEADS):
        lo = h * HEAD_PAD
        q_ref[:, lo:lo + HEAD_PAD] = (q1[:, lo:lo + HEAD_PAD] * cosq
                                      + q2[:, lo:lo + HEAD_PAD] * sinq).astype(BF16)
    ckv_ref[...] = ckv
    kr_ref[...] = kr
    _expand_kv(ckv.astype(BF16), kr * cosk + kr_rot * sink, wk_ref, wvt_ref, k_ref, vt_ref)


def _mla_ctx_kernel(ckv_ref, kr_ref, wk_ref, wvt_ref, k_ref, vt_ref):
    _expand_kv(ckv_ref[...].astype(BF16), kr_ref[...], wk_ref, wvt_ref, k_ref, vt_ref)


def _mla_attn_kernel(*refs, n_parts, heads):
    q_ref = refs[0]
    kv_refs = refs[1:1 + 2 * n_parts]
    o_ref = refs[1 + 2 * n_parts]
    def scores(h):
        qh = q_ref[:, h * HEAD_PAD:(h + 1) * HEAD_PAD]
        return [lax.dot_general(kv_refs[2 * i][:, h * HEAD_PAD:(h + 1) * HEAD_PAD], qh, NT,
                                preferred_element_type=F32) for i in range(n_parts)]

    ss_next = scores(0)
    for h in range(heads):
        ss = ss_next
        if h + 1 < heads:
            ss_next = scores(h + 1)
        mx = ss[0].max(axis=0, keepdims=True)
        for s in ss[1:]:
            mx = jnp.maximum(mx, s.max(axis=0, keepdims=True))
        acc = None
        for i, s in enumerate(ss):
            p = jnp.exp2(s - mx).astype(BF16)
            o = jnp.dot(kv_refs[2 * i + 1][h * VT_ROWS:(h + 1) * VT_ROWS, :], p, preferred_element_type=F32)
            acc = o if acc is None else acc + o
        o_ref[h * V_HEAD:(h + 1) * V_HEAD, :] = (acc[:V_HEAD] / acc[V_HEAD:V_HEAD + 1]).astype(BF16)


def _mla_attention(q, parts, grp, tq, heads):
    nq = grp.length // tq
    qoff = grp.row0 // tq
    in_specs = [pl.BlockSpec((tq, heads * HEAD_PAD), lambda b, p, i: (qoff + b * nq + i, p))]
    args = [q]
    for k, vt, row0, lk in parts:
        koff = row0 // lk
        in_specs.append(pl.BlockSpec((lk, heads * HEAD_PAD), lambda b, p, i, koff=koff: (koff + b, p)))
        in_specs.append(pl.BlockSpec((heads * VT_ROWS, lk), lambda b, p, i, koff=koff: (p, koff + b)))
        args += [k, vt]
    return pl.pallas_call(
        functools.partial(_mla_attn_kernel, n_parts=len(parts), heads=heads),
        grid=(grp.nb, MLA_HEADS // heads, nq),
        in_specs=in_specs,
        out_specs=pl.BlockSpec((heads * V_HEAD, tq), lambda b, p, i: (p, b * nq + i)),
        out_shape=jax.ShapeDtypeStruct((MLA_HEADS * V_HEAD, grp.rows), BF16),
        compiler_params=_cparams(("arbitrary", "arbitrary", "arbitrary")),
        name="mla_attention",
    )(*args)


def _mla_out_kernel(otp_ref, ots_ref, wo_ref, x_ref, mod_ref, o_ref, *, n_prompt_tiles):
    o_t = jnp.where(pl.program_id(0) < n_prompt_tiles, otp_ref[...], ots_ref[...])
    y = lax.dot_general(o_t, wo_ref[...], TN, preferred_element_type=F32)
    o_ref[...] = x_ref[...] + mod_ref[...][2:3] * y


def _rope_tables(groups):
    prompt, sample = groups
    half = QK_ROPE // 2
    freqs = 1.0 / (ROPE_BASE ** (jnp.arange(0, half, 2, dtype=F32) / half))
    pos_i = jnp.arange(sample.length)
    pos = jnp.stack([pos_i // GRID_W, pos_i % GRID_W], axis=-1).astype(F32)
    ang = pos[:, :, None] * freqs
    ang = jnp.broadcast_to(ang[:, :, None, :], (sample.length, 2, 2, half // 2)).reshape(sample.length, QK_ROPE)
    lanes = lambda v: jnp.pad(v, ((0, 0), (QK_NOPE, HEAD_PAD - QK_HEAD)))
    cos = jnp.concatenate([lanes(jnp.ones((prompt.rows, QK_ROPE), F32))] + [lanes(jnp.cos(ang))] * sample.nb, axis=0)
    sin = jnp.concatenate([jnp.zeros((prompt.rows, HEAD_PAD), F32)] + [lanes(jnp.sin(ang))] * sample.nb, axis=0)
    return cos, sin


def _rot_cols(w):
    q = QK_ROPE // 4
    w4 = w.reshape(w.shape[:-1] + (2, 2, q))
    return jnp.stack([-w4[..., 1, :], w4[..., 0, :]], axis=-2).reshape(w.shape)


def _mla_weights(w_in, wq_b, wkv_b):
    d = w_in.shape[0]
    z = lambda *s: jnp.zeros(s, F32)
    k_r = w_in[:, Q_LORA + KV_LORA:]
    pad_tile = lambda c: jnp.concatenate([z(d, QK_NOPE), c, z(d, HEAD_PAD - QK_HEAD)], axis=1)
    w_in_ext = jnp.concatenate([w_in[:, :Q_LORA + KV_LORA], pad_tile(k_r), pad_tile(_rot_cols(k_r))], axis=1)
    wq = wq_b.reshape(Q_LORA, MLA_HEADS, QK_HEAD)
    zq = lambda n: z(Q_LORA, MLA_HEADS, n)
    wq1 = jnp.concatenate([wq, zq(HEAD_PAD - QK_HEAD)], axis=-1)
    wq2 = jnp.concatenate([zq(QK_NOPE), _rot_cols(wq[..., QK_NOPE:]), zq(HEAD_PAD - QK_HEAD)], axis=-1)
    wkv = wkv_b.reshape(KV_LORA, MLA_HEADS, QK_NOPE + V_HEAD)
    wk = jnp.concatenate([wkv[..., :QK_NOPE], z(KV_LORA, MLA_HEADS, HEAD_PAD - QK_NOPE)], axis=-1)
    wv = jnp.concatenate([wkv[..., QK_NOPE:], z(KV_LORA, MLA_HEADS, VT_ROWS - V_HEAD)], axis=-1)
    wvt = wv.reshape(KV_LORA, MLA_HEADS * VT_ROWS).T
    flat = lambda w: w.reshape(w.shape[0], -1).astype(BF16)
    return w_in_ext.astype(BF16), flat(wq1), flat(wq2), flat(wk), wvt.astype(BF16)


def _mla_layer(x, modt, groups, layer, norm_g3, cache, cos, sin, w_in, q_norm, kv_norm, wq_b, wkv_b, wo):
    t = x.shape[0]
    tm = TOKEN_TILE
    slot = 3 * layer + 1
    w_in_ext, wq1, wq2, wk, wvt = _mla_weights(w_in, wq_b, wkv_b)
    hd = MLA_HEADS * HEAD_PAD
    vr = MLA_HEADS * VT_ROWS
    row = lambda c: pl.BlockSpec((tm, c), lambda i: (i, 0))
    q, k, vt, ckv, kr = pl.pallas_call(
        _mla_proj_kernel,
        grid=(t // tm,),
        in_specs=[
            row(D_MODEL), _mod_spec(groups, tm, slot), _const_spec((1, D_MODEL), (slot,)),
            _const_spec(w_in_ext.shape), _const_spec((1, Q_LORA)), _const_spec((1, KV_LORA)),
            _const_spec(wq1.shape), _const_spec(wq2.shape), _const_spec(wk.shape), _const_spec(wvt.shape),
            row(HEAD_PAD), row(HEAD_PAD),
        ],
        out_specs=[row(hd), row(hd), pl.BlockSpec((vr, tm), lambda i: (0, i)), row(KV_LORA), row(HEAD_PAD)],
        out_shape=[jax.ShapeDtypeStruct((t, hd), BF16), jax.ShapeDtypeStruct((t, hd), BF16),
                   jax.ShapeDtypeStruct((vr, t), BF16), jax.ShapeDtypeStruct((t, KV_LORA), F32),
                   jax.ShapeDtypeStruct((t, HEAD_PAD), F32)],
        compiler_params=_cparams(("arbitrary",)),
        name="mla_project",
    )(x, modt, norm_g3, w_in_ext, q_norm.reshape(1, -1), kv_norm.reshape(1, -1),
      wq1, wq2, wk, wvt, cos, sin)

    prompt, sample = groups
    nb, past, _ = cache.shape
    cflat = cache.reshape(nb * past, -1)
    ckv_c = cflat[:, :KV_LORA]
    kr_c = jnp.pad(cflat[:, KV_LORA:], ((0, 0), (QK_NOPE, HEAD_PAD - QK_HEAD)))
    crow = lambda c: pl.BlockSpec((past, c), lambda i: (i, 0))
    k_c, vt_c = pl.pallas_call(
        _mla_ctx_kernel,
        grid=(nb,),
        in_specs=[crow(KV_LORA), crow(HEAD_PAD), _const_spec(wk.shape), _const_spec(wvt.shape)],
        out_specs=[crow(hd), pl.BlockSpec((vr, past), lambda i: (0, i))],
        out_shape=[jax.ShapeDtypeStruct((nb * past, hd), BF16), jax.ShapeDtypeStruct((vr, nb * past), BF16)],
        compiler_params=_cparams(("arbitrary",)),
        name="mla_ctx_expand",
    )(ckv_c, kr_c, wk, wvt)

    ot_p = _mla_attention(q, [(k, vt, prompt.row0, prompt.length)], prompt, 256, MLA_HEADS)
    ot_s = _mla_attention(q, [(k_c, vt_c, 0, past), (k, vt, sample.row0, sample.length)], sample, 512, 8)
    vd = MLA_HEADS * V_HEAD
    o_specs, n_p = _two_group_specs(groups, tm, vd, axis=1)
    x = pl.pallas_call(
        functools.partial(_mla_out_kernel, n_prompt_tiles=n_p),
        grid=(t // tm,),
        in_specs=o_specs + [_const_spec((vd, D_MODEL)), row(D_MODEL), _mod_spec(groups, tm, slot)],
        out_specs=row(D_MODEL),
        out_shape=jax.ShapeDtypeStruct((t, D_MODEL), F32),
        compiler_params=_cparams(("arbitrary",)),
        name="mla_out",
    )(ot_p, ot_s, wo.astype(BF16), x, modt)
    entry = jnp.concatenate([ckv[:prompt.rows], kr[:prompt.rows, QK_NOPE:QK_HEAD]], axis=-1)
    return x, entry.reshape(prompt.nb, prompt.length, -1)


def _ssm_proj_kernel(x_ref, mod_ref, g_ref, wz_ref, wx_ref, wdt_ref, z_ref, xbc_ref, dt_ref):
    m = mod_ref[...]
    hb = _adaln(x_ref[...], g_ref[...], m[0:1], m[1:2]).astype(BF16)
    z_ref[...] = jnp.dot(hb, wz_ref[...], preferred_element_type=F32).astype(BF16)
    xbc_ref[...] = jnp.dot(hb, wx_ref[...], preferred_element_type=F32)
    dt_ref[...] = jnp.dot(hb, wdt_ref[...], preferred_element_type=F32)


def _split3(a):
    hi = a.astype(BF16)
    r1 = a - hi.astype(F32)
    mid = r1.astype(BF16)
    lo = (r1 - mid.astype(F32)).astype(BF16)
    return hi, mid, lo


def _ssd_kernel(*refs, zero_init, emit_state, layer, nc):
    it = iter(refs)
    cur_ref, prev_ref, next_ref, dt_ref = next(it), next(it), next(it), next(it)
    cw_ref, cb_ref, dtb_ref, alog_ref, dskip_ref = next(it), next(it), next(it), next(it), next(it)
    h0_ref = None if zero_init else next(it)
    prev_st_ref = next(it) if layer else None
    y_ref = next(it)
    st_ref = next(it) if emit_state else None
    s_f, s_b, sb_loc, y_keep, c_keep, ecb_keep, etb_keep, xpad, act = (next(it) for _ in range(9))

    q = CHUNK
    step = pl.program_id(1)
    npairs = SSM_HEADS // 2
    hpg = SSM_HEADS // SSM_GROUPS
    lane = lax.broadcasted_iota(jnp.int32, (q, LANES), 1)
    left = lane < SSM_HEADDIM

    def pair_cols(v, p):
        return jnp.where(left[:v.shape[0]], v[:, 2 * p:2 * p + 1], v[:, 2 * p + 1:2 * p + 2])

    @pl.when(step == 0)
    def _init():
        if zero_init:
            s_f[...] = jnp.zeros(s_f.shape, F32)
            s_b[...] = jnp.zeros(s_b.shape, F32)
        else:
            s_f[...] = h0_ref[0]
            s_b[...] = h0_ref[1]

    @pl.when(step < nc)
    def _sweep_up():
        c = step
        blk = 4 * SUBLANES
        st = blk // SUBLANES
        has_prev = c > 0
        has_next = c < nc - 1
        for j in range(CONV_DIM // LANES):
            cols = slice(j * LANES, (j + 1) * LANES)
            xpad[j, 0:SUBLANES, :] = jnp.where(has_prev, prev_ref[:, cols], 0.0)
            xpad[j, SUBLANES:SUBLANES + q, :] = cur_ref[:, cols]
            xpad[j, SUBLANES + q:, :] = jnp.where(has_next, next_ref[:, cols], 0.0)
            wk = [jnp.broadcast_to(cw_ref[k:k + 1, cols], (SUBLANES, LANES)) for k in range(CONV_W)]
            bias = jnp.broadcast_to(cb_ref[:, cols], (SUBLANES, LANES))
            for b in range(q // blk):
                r0 = SUBLANES + b * blk - CONV_W // 2
                taps = [xpad[j, pl.ds(r0 + u, SUBLANES, stride=st), :] for u in range(st + CONV_W - 1)]
                for v in range(st):
                    acc = bias
                    for k in range(CONV_W):
                        acc = acc + taps[v + k] * wk[k]
                    act[j, pl.ds(b * blk + v, SUBLANES, stride=st), :] = _silu(acc)
        n_x = D_INNER // LANES
        for g in range(SSM_GROUPS):
            c_keep[c, :, g * D_STATE:(g + 1) * D_STATE] = act[n_x + SSM_GROUPS + g].astype(BF16)

        dtr = dt_ref[...] + dtb_ref[...]
        dtv = jnp.maximum(dtr, 0.0) + jnp.log1p(jnp.exp(-jnp.abs(dtr)))
        a2 = dtv * (-LOG2E * jnp.exp(alog_ref[...]))
        ri = lax.broadcasted_iota(jnp.int32, (q, q), 0)
        ci = lax.broadcasted_iota(jnp.int32, (q, q), 1)
        lower = ci <= ri
        slower = ci < ri
        supper = ci > ri
        tot = jnp.sum(a2, axis=0, keepdims=True)
        dirs = []
        for d, keep in enumerate((lower, ci >= ri)):
            ad = a2[:, d * LANES:(d + 1) * LANES]
            dtd = dtv[:, d * LANES:(d + 1) * LANES]
            td = tot[:, d * LANES:(d + 1) * LANES]
            tri = keep.astype(BF16)
            cum = sum(jnp.dot(tri, part, preferred_element_type=F32) for part in _split3(ad))
            dirs.append(dict(cum=cum, cum_t=cum.T, dt_t=dtd.T, ecum=jnp.exp2(cum), etot=jnp.exp2(td),
                             wout_t=(jnp.exp2(td - cum) * dtd).T))
        fw, bw = dirs
        dt_sum_t = fw["dt_t"] + bw["dt_t"]
        ecb_keep[c] = bw["ecum"]
        etb_keep[c] = jnp.broadcast_to(bw["etot"], (SUBLANES, LANES))
        dsum = dskip_ref[0:1, :] + dskip_ref[1:2, :]

        for g in range(SSM_GROUPS):
            bm = act[n_x + g]
            cm_b = c_keep[c, :, g * D_STATE:(g + 1) * D_STATE]
            bm_t = bm.T
            cb = lax.dot_general(cm_b, bm.astype(BF16), NT, preferred_element_type=F32)
            for pp in range(hpg // 2):
                p = g * (hpg // 2) + pp
                xp = act[p]
                xbd = jnp.concatenate([jnp.where(left, xp, 0.0), jnp.where(left, 0.0, xp)], axis=0).astype(BF16)
                ms, bfs, bbs = [], [], []
                for h in (2 * p, 2 * p + 1):
                    seg = jnp.where(lower, fw["cum"][:, h:h + 1] - fw["cum_t"][h:h + 1, :],
                                    bw["cum"][:, h:h + 1] - bw["cum_t"][h:h + 1, :])
                    wgt = jnp.where(slower, fw["dt_t"][h:h + 1, :],
                                    jnp.where(supper, bw["dt_t"][h:h + 1, :], dt_sum_t[h:h + 1, :]))
                    ms.append((cb * jnp.exp2(seg) * wgt).astype(BF16))
                    bfs.append((bm_t * fw["wout_t"][h:h + 1, :]).astype(BF16))
                    bbs.append((bm_t * bw["wout_t"][h:h + 1, :]).astype(BF16))
                y = jnp.dot(jnp.concatenate(ms, axis=1), xbd, preferred_element_type=F32)
                st = s_f[p]
                y_off = jnp.dot(cm_b, st.astype(BF16), preferred_element_type=F32)
                y_keep[c, :, p * LANES:(p + 1) * LANES] = (y + y_off * pair_cols(fw["ecum"], p)
                                                        + dsum[:, p * LANES:(p + 1) * LANES] * xp)
                s_f[p] = st * pair_cols(fw["etot"], p) + jnp.dot(jnp.concatenate(bfs, axis=1), xbd,
                                                                 preferred_element_type=F32)
                sb_loc[c, p] = jnp.dot(jnp.concatenate(bbs, axis=1), xbd, preferred_element_type=F32)

    @pl.when(step >= nc)
    def _sweep_down():
        c = 2 * nc - 1 - step
        ecb = ecb_keep[c]
        etb = etb_keep[c][0:1]
        for p in range(npairs):
            g = p // (hpg // 2)
            st = s_b[p]
            y_off = jnp.dot(c_keep[c, :, g * D_STATE:(g + 1) * D_STATE], st.astype(BF16),
                            preferred_element_type=F32)
            y_ref[:, p * LANES:(p + 1) * LANES] = (y_keep[c, :, p * LANES:(p + 1) * LANES]
                                                    + y_off * pair_cols(ecb, p))
            s_b[p] = st * pair_cols(etb, p) + sb_loc[c, p]

        if emit_state:
            @pl.when(step == 2 * nc - 1)
            def _emit():
                for l in range(layer):
                    st_ref[l] = prev_st_ref[l]
                for p in range(npairs):
                    st_ref[layer, 0, p] = s_f[p].T
                    st_ref[layer, 1, p] = s_b[p].T


def _ssd_scan(xbc, dt, grp, conv_w, conv_b, dt_bias, a_log, d_skip, h0, state_out):
    t = xbc.shape[0]
    q = CHUNK
    nc = grp.length // q
    zero_init = h0 is None
    emit_state = state_out is not None
    boff = grp.row0 // q
    npairs = SSM_HEADS // 2
    qs = q // SUBLANES

    chunk = lambda b, s: boff + b * nc + jnp.minimum(s, nc - 1)
    in_specs = [
        pl.BlockSpec((q, CONV_DIM), lambda b, s: (chunk(b, s), 0)),
        pl.BlockSpec((SUBLANES, CONV_DIM), lambda b, s: (jnp.maximum(chunk(b, s) * qs - 1, 0), 0)),
        pl.BlockSpec((SUBLANES, CONV_DIM), lambda b, s: (jnp.minimum((chunk(b, s) + 1) * qs, t // SUBLANES - 1), 0)),
        pl.BlockSpec((q, 2 * LANES), lambda b, s: (chunk(b, s), 0)),
        _const_spec((SUBLANES, CONV_DIM)), _const_spec((1, CONV_DIM)),
        _const_spec((1, 2 * LANES)), _const_spec((1, 2 * LANES)), _const_spec((2, D_INNER)),
    ]
    args = [xbc, xbc, xbc, dt, conv_w, conv_b, dt_bias, a_log, d_skip]
    if not zero_init:
        in_specs.append(pl.BlockSpec((None, 2, npairs, D_STATE, LANES), lambda b, s: (b, 0, 0, 0, 0)))
        args.append(h0)
    state_block = lambda n: pl.BlockSpec((None, n, 2, npairs, LANES, D_STATE), lambda b, s: (b, 0, 0, 0, 0, 0))
    layer = state_out[1] if emit_state else 0
    if layer:
        in_specs.append(state_block(layer))
        args.append(state_out[0])
    out_specs = [
        pl.BlockSpec((q, D_INNER), lambda b, s: (b * nc + jnp.where(s < nc, nc - 1, 2 * nc - 1 - s), 0)),
    ]
    out_shape = [jax.ShapeDtypeStruct((grp.rows, D_INNER), F32)]
    if emit_state:
        out_specs.append(state_block(layer + 1))
        out_shape.append(jax.ShapeDtypeStruct((grp.nb, layer + 1, 2, npairs, LANES, D_STATE), F32))
    return pl.pallas_call(
        functools.partial(_ssd_kernel, zero_init=zero_init, emit_state=emit_state, layer=layer, nc=nc),
        grid=(grp.nb, 2 * nc),
        in_specs=in_specs,
        out_specs=out_specs,
        out_shape=out_shape,
        scratch_shapes=[pltpu.VMEM((npairs, D_STATE, LANES), F32),
                        pltpu.VMEM((npairs, D_STATE, LANES), F32),
                        pltpu.VMEM((nc, npairs, D_STATE, LANES), F32),
                        pltpu.VMEM((nc, q, D_INNER), F32),
                        pltpu.VMEM((nc, q, SSM_GROUPS * D_STATE), BF16),
                        pltpu.VMEM((nc, q, LANES), F32),
                        pltpu.VMEM((nc, SUBLANES, LANES), F32),
                        pltpu.VMEM((CONV_DIM // LANES, q + 2 * SUBLANES, LANES), F32),
                        pltpu.VMEM((CONV_DIM // LANES, q, LANES), F32)],
        compiler_params=_cparams(("arbitrary", "arbitrary")),
        name="ssd_scan",
    )(*args)


def _ssm_out_kernel(yp_ref, ys_ref, z_ref, ng_ref, w_ref, x_ref, mod_ref, o_ref, *, n_prompt_tiles):
    y = jnp.where(pl.program_id(0) < n_prompt_tiles, yp_ref[...], ys_ref[...])
    y = y * _silu(z_ref[...].astype(F32))
    yn = _rmsnorm(y, ng_ref[...]).astype(BF16)
    o_ref[...] = x_ref[...] + mod_ref[...][2:3] * jnp.dot(yn, w_ref[...], preferred_element_type=F32)


def _pair_state(h):
    b = h.shape[0]
    return h.reshape(b, 2, SSM_HEADS // 2, 2, SSM_HEADDIM, D_STATE).transpose(0, 1, 2, 5, 3, 4).reshape(
        b, 2, SSM_HEADS // 2, D_STATE, 2 * SSM_HEADDIM)


def _ssm_layer(x, modt, groups, layer, norm_g3, h0_sample, state_out, w_z, w_x, w_dt, conv_w, conv_b,
               dt_bias, a_log, d_skip, norm_g, w_out):
    t = x.shape[0]
    tm = TOKEN_TILE
    slot = 3 * layer + 1
    row = lambda c: pl.BlockSpec((tm, c), lambda i: (i, 0))
    z, xbc, dt = pl.pallas_call(
        _ssm_proj_kernel,
        grid=(t // tm,),
        in_specs=[row(D_MODEL), _mod_spec(groups, tm, slot), _const_spec((1, D_MODEL), (slot,)),
                  _const_spec(w_z.shape), _const_spec(w_x.shape), _const_spec(w_dt.shape)],
        out_specs=[row(D_INNER), row(CONV_DIM), row(2 * LANES)],
        out_shape=[jax.ShapeDtypeStruct((t, D_INNER), BF16), jax.ShapeDtypeStruct((t, CONV_DIM), F32),
                   jax.ShapeDtypeStruct((t, 2 * LANES), F32)],
        compiler_params=_cparams(("arbitrary",)),
        name="ssm_project",
    )(x, modt, norm_g3, w_z, w_x, w_dt)

    lane_pad = lambda v: jnp.pad(v, ((0, 0), (0, LANES - v.shape[-1]))).reshape(1, 2 * LANES)
    cw = jnp.pad(conv_w, ((0, SUBLANES - CONV_W), (0, 0)))
    cb = conv_b.reshape(1, CONV_DIM)
    dsk = jnp.repeat(d_skip, SSM_HEADDIM, axis=-1)
    prompt, sample = groups
    scan = functools.partial(_ssd_scan, xbc, dt)
    yp, new_state = scan(prompt, cw, cb, lane_pad(dt_bias), lane_pad(a_log), dsk, None, state_out)
    ys, = scan(sample, cw, cb, lane_pad(dt_bias), lane_pad(a_log), dsk, _pair_state(h0_sample), None)

    to = TOKEN_TILE
    row = lambda c: pl.BlockSpec((to, c), lambda i: (i, 0))
    y_specs, n_p = _two_group_specs(groups, to, D_INNER)
    x = pl.pallas_call(
        functools.partial(_ssm_out_kernel, n_prompt_tiles=n_p),
        grid=(t // to,),
        in_specs=y_specs + [row(D_INNER), _const_spec((1, D_INNER)),
                  _const_spec((D_INNER, D_MODEL)), row(D_MODEL), _mod_spec(groups, to, slot)],
        out_specs=row(D_MODEL),
        out_shape=jax.ShapeDtypeStruct((t, D_MODEL), F32),
        compiler_params=_cparams(("arbitrary",)),
        name="ssm_out",
    )(yp, ys, z, norm_g.reshape(1, D_INNER), w_out, x, modt)
    return x, new_state


def kernel(x_prompt, x_sample, cache_mla, state_ssm, c, c_ctx, mod_w, mod_b, norm_g, ffn_w_in, ffn_w_out,
           mla_w_in, mla_q_norm, mla_kv_norm, mla_wq_b, mla_wkv_b, mla_wo, ssm_w_in, ssm_conv_w, ssm_conv_b,
           ssm_dt_bias, ssm_a_log, ssm_d, ssm_norm_g, ssm_w_out, final_norm_g):
    nbp, lp, d = x_prompt.shape
    nbs, ls, _ = x_sample.shape
    prompt = _Group(0, nbp, lp)
    sample = _Group(nbp * lp, nbs, ls)
    groups = (prompt, sample)
    x = jnp.concatenate([x_prompt.reshape(-1, d), x_sample.reshape(-1, d)], axis=0)

    ncond = 1 + nbs
    sc = jnp.concatenate([c_ctx[None, :], c, jnp.zeros((SUBLANES - ncond, d), F32)], axis=0)
    mod = _modulation(sc, mod_w, mod_b)
    modt = mod[:, :ncond].reshape(DEPTH, ncond, 3, 3, d).transpose(1, 0, 2, 3, 4).reshape(ncond, DEPTH * 3, 3, d)

    norm_g3 = norm_g.reshape(DEPTH * 3, 1, d)
    ffn_in = ffn_w_in
    ffn_out = ffn_w_out
    n_ssm = ssm_w_in.shape[0]
    ssm_wz = ssm_w_in[:, :, :D_INNER].astype(BF16)
    ssm_wx = ssm_w_in[:, :, D_INNER:D_INNER + CONV_DIM].astype(BF16)
    wdt = ssm_w_in[:, :, D_INNER + CONV_DIM:].reshape(n_ssm, d, 2, SSM_HEADS)
    ssm_wdt = jnp.pad(wdt, ((0, 0), (0, 0), (0, 0), (0, LANES - SSM_HEADS))).reshape(n_ssm, d, 2 * LANES).astype(BF16)
    ssm_wo = ssm_w_out.astype(BF16)

    cos, sin = _rope_tables(groups)
    new_mla = []
    new_state = None
    tiles_p = prompt.rows // TOKEN_TILE
    tiles_s = sample.rows // TOKEN_TILE
    for i in range(DEPTH):
        j = i // 2
        x = _ffn_half(x, modt, groups, i, 0, norm_g3, ffn_in, ffn_out, final_norm_g)
        if i % 2 == 0:
            x, entry = _mla_layer(x, modt, groups, i, norm_g3, cache_mla[:, j], cos, sin,
                                  mla_w_in[j], mla_q_norm[j], mla_kv_norm[j], mla_wq_b[j], mla_wkv_b[j], mla_wo[j])
            new_mla.append(entry)
        else:
            x, new_state = _ssm_layer(x, modt, groups, i, norm_g3, state_ssm[:, j], (new_state, j),
                                      ssm_wz[j], ssm_wx[j], ssm_wdt[j], ssm_conv_w[j], ssm_conv_b[j],
                                      ssm_dt_bias[j], ssm_a_log[j], ssm_d[j], ssm_norm_g[j], ssm_wo[j])
        if i < DEPTH - 1:
            x = _ffn_half(x, modt, groups, i, 1, norm_g3, ffn_in, ffn_out, final_norm_g)
    last = functools.partial(_ffn_half, x, modt, groups, DEPTH - 1, 1, norm_g3, ffn_in, ffn_out, final_norm_g, True)
    y_prompt = last((0, tiles_p)).reshape(nbp, lp, d)
    y_sample = last((tiles_p, tiles_s)).reshape(nbs, ls, d)
    new_state = new_state.reshape(nbp, n_ssm, 2, SSM_HEADS, SSM_HEADDIM, D_STATE)
    return y_prompt, y_sample, jnp.stack(new_mla, axis=1), new_state
```

```python
import functools
import math

import jax
import jax.numpy as jnp
from jax import lax
from jax.experimental import pallas as pl
from jax.experimental.pallas import tpu as pltpu

F32 = jnp.float32
BF16 = jnp.bfloat16

D_MODEL = 1024
DEPTH = 4
N_MOD = 9
D_FF = 2816
GRID_W = 64
MLA_HEADS = 16
Q_LORA = 512
KV_LORA = 256
QK_NOPE = 64
QK_ROPE = 32
V_HEAD = 64
QK_HEAD = QK_NOPE + QK_ROPE
ROPE_BASE = 10000.0
HEAD_PAD = 128
VT_ROWS = 80
D_INNER = 2048
SSM_HEADDIM = 64
SSM_HEADS = 32
SSM_GROUPS = 4
D_STATE = 128
CONV_W = 5
CONV_DIM = D_INNER + 2 * SSM_GROUPS * D_STATE
CHUNK = 128
EPS = 1e-6
LOG2E = math.log2(math.e)

LANES = 128
SUBLANES = 8
TOKEN_TILE = 512
FFN_TILE = 1024
FF_CHUNK = 256
FFN_STAGE_SLOTS = 3
VMEM_LIMIT = 56 * 1024 * 1024

NT = (((1,), (1,)), ((), ()))
TN = (((0,), (0,)), ((), ()))


def _cparams(sem):
    return pltpu.CompilerParams(dimension_semantics=sem, vmem_limit_bytes=VMEM_LIMIT)


def _const_spec(shape, index=None):
    index = tuple(index or ())
    block = (None,) * len(index) + tuple(shape)
    zeros = (0,) * len(shape)
    return pl.BlockSpec(block, lambda *_: index + zeros, pipeline_mode=pl.Buffered(1))


def _rmsnorm(x, g):
    ms = jnp.mean(x * x, axis=-1, keepdims=True)
    return (x * lax.rsqrt(ms + EPS)) * g


def _silu(x):
    h = 0.5 * x
    return h * jnp.tanh(h) + h


def _adaln(x, g, shift, scale):
    return _rmsnorm(x, g) * (1.0 + scale) + shift


class _Group:
    def __init__(self, row0, nb, length):
        self.row0, self.nb, self.length = row0, nb, length

    @property
    def rows(self):
        return self.nb * self.length


def _mod_kernel(sc_ref, w_ref, b_ref, o_ref):
    s = _silu(sc_ref[...])
    o_ref[...] = jnp.dot(s.astype(BF16), w_ref[...].astype(BF16),
                         preferred_element_type=F32) + b_ref[...]


def _modulation(sc, mod_w, mod_b):
    tn = 1024
    n = N_MOD * D_MODEL
    return pl.pallas_call(
        _mod_kernel,
        grid=(DEPTH, n // tn),
        in_specs=[
            pl.BlockSpec((SUBLANES, D_MODEL), lambda i, j: (0, 0)),
            pl.BlockSpec((None, D_MODEL, tn), lambda i, j: (i, 0, j)),
            pl.BlockSpec((None, 1, tn), lambda i, j: (i, 0, j)),
        ],
        out_specs=pl.BlockSpec((None, SUBLANES, tn), lambda i, j: (i, 0, j)),
        out_shape=jax.ShapeDtypeStruct((DEPTH, SUBLANES, n), F32),
        compiler_params=_cparams(("arbitrary", "arbitrary")),
        name="modulation",
    )(sc, mod_w, mod_b.reshape(DEPTH, 1, n))


def _mod_spec(groups, tile, slot, first_tile=0):
    bounds = []
    for g in groups:
        for b in range(g.nb if g.row0 else 1):
            bounds.append((g.row0 + b * g.length) // tile)

    def index_map(i, *_):
        gidx = 0
        for lo in bounds[1:]:
            gidx = gidx + (i + first_tile >= lo).astype(jnp.int32)
        return (gidx, slot, 0, 0)

    return pl.BlockSpec((None, None, 3, D_MODEL), index_map)


def _two_group_specs(groups, tile, width, axis=0):
    prompt, sample = groups
    n_p = prompt.rows // tile
    n_s = sample.rows // tile
    if axis == 0:
        return [pl.BlockSpec((tile, width), lambda i: (jnp.minimum(i, n_p - 1), 0)),
                pl.BlockSpec((tile, width), lambda i: (jnp.clip(i - n_p, 0, n_s - 1), 0))], n_p
    return [pl.BlockSpec((width, tile), lambda i: (0, jnp.minimum(i, n_p - 1))),
            pl.BlockSpec((width, tile), lambda i: (0, jnp.clip(i - n_p, 0, n_s - 1)))], n_p


def _ffn_kernel(x_ref, mod_ref, g_ref, fg_ref, win_hbm, wout_hbm, o_ref,
                win_b, wout_b, a_scr, stage_in, stage_out, sem, *, layer, half, final_norm):
    n_chunks = D_FF // FF_CHUNK
    n_slots = stage_out.shape[0]
    ahead = n_slots - 1

    def chunk_copies(c):
        lo = c * FF_CHUNK
        slot = c % n_slots
        w_in = win_hbm.at[layer, half]
        return (pltpu.make_async_copy(w_in.at[:, pl.ds(lo, FF_CHUNK)], stage_in.at[slot, 0], sem.at[slot, 0]),
                pltpu.make_async_copy(w_in.at[:, pl.ds(D_FF + lo, FF_CHUNK)], stage_in.at[slot, 1], sem.at[slot, 1]),
                pltpu.make_async_copy(wout_hbm.at[layer, half, pl.ds(lo, FF_CHUNK), :], stage_out.at[slot],
                                      sem.at[slot, 2]))

    m = mod_ref[...]

    def rows_pass(r0, r1, before_chunk=None):
        x = x_ref[r0:r1, :]
        hb = _adaln(x, g_ref[...], m[0:1], m[1:2]).astype(BF16)
        for c in range(n_chunks):
            if before_chunk is not None:
                before_chunk(c)
            lo = c * FF_CHUNK
            gate = jnp.dot(hb, win_b[:, lo:lo + FF_CHUNK], preferred_element_type=F32)
            up = jnp.dot(hb, win_b[:, D_FF + lo:D_FF + lo + FF_CHUNK], preferred_element_type=F32)
            a_scr[r0:r1, lo:lo + FF_CHUNK] = (_silu(gate) * up).astype(BF16)
        y = jnp.dot(a_scr[r0:r1, :], wout_b[...], preferred_element_type=F32)
        out = x + (0.5 * m[2:3]) * y
        if final_norm:
            out = _rmsnorm(out, fg_ref[...])
        o_ref[r0:r1, :] = out

    def land_chunk(c):
        if c + ahead < n_chunks:
            for cp in chunk_copies(c + ahead):
                cp.start()
        for cp in chunk_copies(c):
            cp.wait()
        lo = c * FF_CHUNK
        slot = c % n_slots
        win_b[:, lo:lo + FF_CHUNK] = stage_in[slot, 0].astype(BF16)
        win_b[:, D_FF + lo:D_FF + lo + FF_CHUNK] = stage_in[slot, 1].astype(BF16)
        wout_b[lo:lo + FF_CHUNK, :] = stage_out[slot].astype(BF16)

    tm = x_ref.shape[0]

    @pl.when(pl.program_id(0) == 0)
    def _first_tile():
        for c in range(ahead):
            for cp in chunk_copies(c):
                cp.start()
        rows_pass(0, tm, land_chunk)

    @pl.when(pl.program_id(0) > 0)
    def _other_tiles():
        rows_pass(0, tm // 2)
        rows_pass(tm // 2, tm)


def _ffn_half(x, modt, groups, layer, half, norm_g3, w_in, w_out, final_g, final_norm=False, tiles=None):
    tm = FFN_TILE
    first, count = tiles or (0, x.shape[0] // tm)
    slot = 3 * layer + 2 * half
    return pl.pallas_call(
        functools.partial(_ffn_kernel, layer=layer, half=half, final_norm=final_norm),
        grid=(count,),
        in_specs=[
            pl.BlockSpec((tm, D_MODEL), lambda i: (i + first, 0)),
            _mod_spec(groups, tm, slot, first),
            _const_spec((1, D_MODEL), (slot,)),
            _const_spec((1, D_MODEL)),
            pl.BlockSpec(memory_space=pl.ANY),
            pl.BlockSpec(memory_space=pl.ANY),
        ],
        out_specs=pl.BlockSpec((tm, D_MODEL), lambda i: (i, 0)),
        out_shape=jax.ShapeDtypeStruct((count * tm, D_MODEL), F32),
        scratch_shapes=[pltpu.VMEM((D_MODEL, 2 * D_FF), BF16),
                        pltpu.VMEM((D_FF, D_MODEL), BF16),
                        pltpu.VMEM((tm, D_FF), BF16),
                        pltpu.VMEM((FFN_STAGE_SLOTS, 2, D_MODEL, FF_CHUNK), F32),
                        pltpu.VMEM((FFN_STAGE_SLOTS, FF_CHUNK, D_MODEL), F32),
                        pltpu.SemaphoreType.DMA((FFN_STAGE_SLOTS, 3))],
        compiler_params=_cparams(("arbitrary",)),
        name="ffn_half",
    )(x, modt, norm_g3, final_g.reshape(1, D_MODEL), w_in, w_out)


def _expand_kv(ckv_b, krope, wk_ref, wvt_ref, k_ref, vt_ref):
    kn = jnp.dot(ckv_b, wk_ref[...], preferred_element_type=F32)
    for h in range(MLA_HEADS):
        lo = h * HEAD_PAD
        k_ref[:, lo:lo + HEAD_PAD] = (kn[:, lo:lo + HEAD_PAD] + krope).astype(BF16)
    vt_ref[...] = lax.dot_general(wvt_ref[...], ckv_b, NT, preferred_element_type=F32).astype(BF16)
    ones = jnp.ones((VT_ROWS - V_HEAD, ckv_b.shape[0]), BF16)
    for h in range(MLA_HEADS):
        vt_ref[h * VT_ROWS + V_HEAD:(h + 1) * VT_ROWS, :] = ones


def _mla_proj_kernel(x_ref, mod_ref, g_ref, win_ref, qn_ref, kvn_ref, wq1_ref, wq2_ref,
                     wk_ref, wvt_ref, cos_ref, sin_ref,
                     q_ref, k_ref, vt_ref, ckv_ref, kr_ref):
    x = x_ref[...]
    m = mod_ref[...]
    hb = _adaln(x, g_ref[...], m[0:1], m[1:2]).astype(BF16)
    a = jnp.dot(hb, win_ref[...], preferred_element_type=F32)
    qa = _rmsnorm(a[:, :Q_LORA], qn_ref[...]).astype(BF16)
    ckv = _rmsnorm(a[:, Q_LORA:Q_LORA + KV_LORA], kvn_ref[...])
    kr = a[:, Q_LORA + KV_LORA:Q_LORA + KV_LORA + HEAD_PAD]
    kr_rot = a[:, Q_LORA + KV_LORA + HEAD_PAD:]
    cosk = cos_ref[...]
    sink = sin_ref[...]
    lane = lax.broadcasted_iota(jnp.int32, cosk.shape, 1)
    scale = LOG2E / math.sqrt(QK_HEAD)
    cosq = (cosk + (lane < QK_NOPE).astype(F32)) * scale
    sinq = sink * scale
    q1 = jnp.dot(qa, wq1_ref[...], preferred_element_type=F32)
    q2 = jnp.dot(qa, wq2_ref[...], preferred_element_type=F32)
    for h in range(MLA_HEADS):
        lo = h * HEAD_PAD
        q_ref[:, lo:lo + HEAD_PAD] = (q1[:, lo:lo + HEAD_PAD] * cosq
                                      + q2[:, lo:lo + HEAD_PAD] * sinq).astype(BF16)
    ckv_ref[...] = ckv
    kr_ref[...] = kr
    _expand_kv(ckv.astype(BF16), kr * cosk + kr_rot * sink, wk_ref, wvt_ref, k_ref, vt_ref)


def _mla_ctx_kernel(ckv_ref, kr_ref, wk_ref, wvt_ref, k_ref, vt_ref):
    _expand_kv(ckv_ref[...].astype(BF16), kr_ref[...], wk_ref, wvt_ref, k_ref, vt_ref)


def _mla_attn_kernel(*refs, n_parts, heads):
    q_ref = refs[0]
    kv_refs = refs[1:1 + 2 * n_parts]
    o_ref = refs[1 + 2 * n_parts]
    def scores(h):
        qh = q_ref[:, h * HEAD_PAD:(h + 1) * HEAD_PAD]
        return [lax.dot_general(kv_refs[2 * i][:, h * HEAD_PAD:(h + 1) * HEAD_PAD], qh, NT,
                                preferred_element_type=F32) for i in range(n_parts)]

    ss_next = scores(0)
    for h in range(heads):
        ss = ss_next
        if h + 1 < heads:
            ss_next = scores(h + 1)
        mx = ss[0].max(axis=0, keepdims=True)
        for s in ss[1:]:
            mx = jnp.maximum(mx, s.max(axis=0, keepdims=True))
        acc = None
        for i, s in enumerate(ss):
            p = jnp.exp2(s - mx).astype(BF16)
            o = jnp.dot(kv_refs[2 * i + 1][h * VT_ROWS:(h + 1) * VT_ROWS, :], p, preferred_element_type=F32)
            acc = o if acc is None else acc + o
        o_ref[h * V_HEAD:(h + 1) * V_HEAD, :] = (acc[:V_HEAD] / acc[V_HEAD:V_HEAD + 1]).astype(BF16)


def _mla_attention(q, parts, grp, tq, heads):
    nq = grp.length // tq
    qoff = grp.row0 // tq
    in_specs = [pl.BlockSpec((tq, heads * HEAD_PAD), lambda b, p, i: (qoff + b * nq + i, p))]
    args = [q]
    for k, vt, row0, lk in parts:
        koff = row0 // lk
        in_specs.append(pl.BlockSpec((lk, heads * HEAD_PAD), lambda b, p, i, koff=koff: (koff + b, p)))
        in_specs.append(pl.BlockSpec((heads * VT_ROWS, lk), lambda b, p, i, koff=koff: (p, koff + b)))
        args += [k, vt]
    return pl.pallas_call(
        functools.partial(_mla_attn_kernel, n_parts=len(parts), heads=heads),
        grid=(grp.nb, MLA_HEADS // heads, nq),
        in_specs=in_specs,
        out_specs=pl.BlockSpec((heads * V_HEAD, tq), lambda b, p, i: (p, b * nq + i)),
        out_shape=jax.ShapeDtypeStruct((MLA_HEADS * V_HEAD, grp.rows), BF16),
        compiler_params=_cparams(("arbitrary", "arbitrary", "arbitrary")),
        name="mla_attention",
    )(*args)


def _mla_out_kernel(otp_ref, ots_ref, wo_ref, x_ref, mod_ref, o_ref, *, n_prompt_tiles):
    o_t = jnp.where(pl.program_id(0) < n_prompt_tiles, otp_ref[...], ots_ref[...])
    y = lax.dot_general(o_t, wo_ref[...], TN, preferred_element_type=F32)
    o_ref[...] = x_ref[...] + mod_ref[...][2:3] * y


def _rope_tables(groups):
    prompt, sample = groups
    half = QK_ROPE // 2
    freqs = 1.0 / (ROPE_BASE ** (jnp.arange(0, half, 2, dtype=F32) / half))
    pos_i = jnp.arange(sample.length)
    pos = jnp.stack([pos_i // GRID_W, pos_i % GRID_W], axis=-1).astype(F32)
    ang = pos[:, :, None] * freqs
    ang = jnp.broadcast_to(ang[:, :, None, :], (sample.length, 2, 2, half // 2)).reshape(sample.length, QK_ROPE)
    lanes = lambda v: jnp.pad(v, ((0, 0), (QK_NOPE, HEAD_PAD - QK_HEAD)))
    cos = jnp.concatenate([lanes(jnp.ones((prompt.rows, QK_ROPE), F32))] + [lanes(jnp.cos(ang))] * sample.nb, axis=0)
    sin = jnp.concatenate([jnp.zeros((prompt.rows, HEAD_PAD), F32)] + [lanes(jnp.sin(ang))] * sample.nb, axis=0)
    return cos, sin


def _rot_cols(w):
    q = QK_ROPE // 4
    w4 = w.reshape(w.shape[:-1] + (2, 2, q))
    return jnp.stack([-w4[..., 1, :], w4[..., 0, :]], axis=-2).reshape(w.shape)


def _mla_weights(w_in, wq_b, wkv_b):
    d = w_in.shape[0]
    z = lambda *s: jnp.zeros(s, F32)
    k_r = w_in[:, Q_LORA + KV_LORA:]
    pad_tile = lambda c: jnp.concatenate([z(d, QK_NOPE), c, z(d, HEAD_PAD - QK_HEAD)], axis=1)
    w_in_ext = jnp.concatenate([w_in[:, :Q_LORA + KV_LORA], pad_tile(k_r), pad_tile(_rot_cols(k_r))], axis=1)
    wq = wq_b.reshape(Q_LORA, MLA_HEADS, QK_HEAD)
    zq = lambda n: z(Q_LORA, MLA_HEADS, n)
    wq1 = jnp.concatenate([wq, zq(HEAD_PAD - QK_HEAD)], axis=-1)
    wq2 = jnp.concatenate([zq(QK_NOPE), _rot_cols(wq[..., QK_NOPE:]), zq(HEAD_PAD - QK_HEAD)], axis=-1)
    wkv = wkv_b.reshape(KV_LORA, MLA_HEADS, QK_NOPE + V_HEAD)
    wk = jnp.concatenate([wkv[..., :QK_NOPE], z(KV_LORA, MLA_HEADS, HEAD_PAD - QK_NOPE)], axis=-1)
    wv = jnp.concatenate([wkv[..., QK_NOPE:], z(KV_LORA, MLA_HEADS, VT_ROWS - V_HEAD)], axis=-1)
    wvt = wv.reshape(KV_LORA, MLA_HEADS * VT_ROWS).T
    flat = lambda w: w.reshape(w.shape[0], -1).astype(BF16)
    return w_in_ext.astype(BF16), flat(wq1), flat(wq2), flat(wk), wvt.astype(BF16)


def _mla_layer(x, modt, groups, layer, norm_g3, cache, cos, sin, w_in, q_norm, kv_norm, wq_b, wkv_b, wo):
    t = x.shape[0]
    tm = TOKEN_TILE
    slot = 3 * layer + 1
    w_in_ext, wq1, wq2, wk, wvt = _mla_weights(w_in, wq_b, wkv_b)
    hd = MLA_HEADS * HEAD_PAD
    vr = MLA_HEADS * VT_ROWS
    row = lambda c: pl.BlockSpec((tm, c), lambda i: (i, 0))
    q, k, vt, ckv, kr = pl.pallas_call(
        _mla_proj_kernel,
        grid=(t // tm,),
        in_specs=[
            row(D_MODEL), _mod_spec(groups, tm, slot), _const_spec((1, D_MODEL), (slot,)),
            _const_spec(w_in_ext.shape), _const_spec((1, Q_LORA)), _const_spec((1, KV_LORA)),
            _const_spec(wq1.shape), _const_spec(wq2.shape), _const_spec(wk.shape), _const_spec(wvt.shape),
            row(HEAD_PAD), row(HEAD_PAD),
        ],
        out_specs=[row(hd), row(hd), pl.BlockSpec((vr, tm), lambda i: (0, i)), row(KV_LORA), row(HEAD_PAD)],
        out_shape=[jax.ShapeDtypeStruct((t, hd), BF16), jax.ShapeDtypeStruct((t, hd), BF16),
                   jax.ShapeDtypeStruct((vr, t), BF16), jax.ShapeDtypeStruct((t, KV_LORA), F32),
                   jax.ShapeDtypeStruct((t, HEAD_PAD), F32)],
        compiler_params=_cparams(("arbitrary",)),
        name="mla_project",
    )(x, modt, norm_g3, w_in_ext, q_norm.reshape(1, -1), kv_norm.reshape(1, -1),
      wq1, wq2, wk, wvt, cos, sin)

    prompt, sample = groups
    nb, past, _ = cache.shape
    cflat = cache.reshape(nb * past, -1)
    ckv_c = cflat[:, :KV_LORA]
    kr_c = jnp.pad(cflat[:, KV_LORA:], ((0, 0), (QK_NOPE, HEAD_PAD - QK_HEAD)))
    crow = lambda c: pl.BlockSpec((past, c), lambda i: (i, 0))
    k_c, vt_c = pl.pallas_call(
        _mla_ctx_kernel,
        grid=(nb,),
        in_specs=[crow(KV_LORA), crow(HEAD_PAD), _const_spec(wk.shape), _const_spec(wvt.shape)],
        out_specs=[crow(hd), pl.BlockSpec((vr, past), lambda i: (0, i))],
        out_shape=[jax.ShapeDtypeStruct((nb * past, hd), BF16), jax.ShapeDtypeStruct((vr, nb * past), BF16)],
        compiler_params=_cparams(("arbitrary",)),
        name="mla_ctx_expand",
    )(ckv_c, kr_c, wk, wvt)

    ot_p = _mla_attention(q, [(k, vt, prompt.row0, prompt.length)], prompt, 256, MLA_HEADS)
    ot_s = _mla_attention(q, [(k_c, vt_c, 0, past), (k, vt, sample.row0, sample.length)], sample, 512, 8)
    vd = MLA_HEADS * V_HEAD
    o_specs, n_p = _two_group_specs(groups, tm, vd, axis=1)
    x = pl.pallas_call(
        functools.partial(_mla_out_kernel, n_prompt_tiles=n_p),
        grid=(t // tm,),
        in_specs=o_specs + [_const_spec((vd, D_MODEL)), row(D_MODEL), _mod_spec(groups, tm, slot)],
        out_specs=row(D_MODEL),
        out_shape=jax.ShapeDtypeStruct((t, D_MODEL), F32),
        compiler_params=_cparams(("arbitrary",)),
        name="mla_out",
    )(ot_p, ot_s, wo.astype(BF16), x, modt)
    entry = jnp.concatenate([ckv[:prompt.rows], kr[:prompt.rows, QK_NOPE:QK_HEAD]], axis=-1)
    return x, entry.reshape(prompt.nb, prompt.length, -1)


def _ssm_proj_kernel(x_ref, mod_ref, g_ref, wz_ref, wx_ref, wdt_ref, z_ref, xbc_ref, dt_ref):
    m = mod_ref[...]
    hb = _adaln(x_ref[...], g_ref[...], m[0:1], m[1:2]).astype(BF16)
    z_ref[...] = jnp.dot(hb, wz_ref[...], preferred_element_type=F32).astype(BF16)
    xbc_ref[...] = jnp.dot(hb, wx_ref[...], preferred_element_type=F32)
    dt_ref[...] = jnp.dot(hb, wdt_ref[...], preferred_element_type=F32)


def _split3(a):
    hi = a.astype(BF16)
    r1 = a - hi.astype(F32)
    mid = r1.astype(BF16)
    lo = (r1 - mid.astype(F32)).astype(BF16)
    return hi, mid, lo


def _ssd_kernel(*refs, zero_init, emit_state, layer, nc):
    it = iter(refs)
    cur_ref, prev_ref, next_ref, dt_ref = next(it), next(it), next(it), next(it)
    cw_ref, cb_ref, dtb_ref, alog_ref, dskip_ref = next(it), next(it), next(it), next(it), next(it)
    h0_ref = None if zero_init else next(it)
    prev_st_ref = next(it) if layer else None
    y_ref = next(it)
    st_ref = next(it) if emit_state else None
    s_f, s_b, sb_loc, y_keep, c_keep, ecb_keep, etb_keep, xpad, act = (next(it) for _ in range(9))

    q = CHUNK
    step = pl.program_id(1)
    npairs = SSM_HEADS // 2
    hpg = SSM_HEADS // SSM_GROUPS
    lane = lax.broadcasted_iota(jnp.int32, (q, LANES), 1)
    left = lane < SSM_HEADDIM

    def pair_cols(v, p):
        return jnp.where(left[:v.shape[0]], v[:, 2 * p:2 * p + 1], v[:, 2 * p + 1:2 * p + 2])

    @pl.when(step == 0)
    def _init():
        if zero_init:
            s_f[...] = jnp.zeros(s_f.shape, F32)
            s_b[...] = jnp.zeros(s_b.shape, F32)
        else:
            s_f[...] = h0_ref[0]
            s_b[...] = h0_ref[1]

    @pl.when(step < nc)
    def _sweep_up():
        c = step
        blk = 4 * SUBLANES
        st = blk // SUBLANES
        has_prev = c > 0
        has_next = c < nc - 1
        for j in range(CONV_DIM // LANES):
            cols = slice(j * LANES, (j + 1) * LANES)
            xpad[j, 0:SUBLANES, :] = jnp.where(has_prev, prev_ref[:, cols], 0.0)
            xpad[j, SUBLANES:SUBLANES + q, :] = cur_ref[:, cols]
            xpad[j, SUBLANES + q:, :] = jnp.where(has_next, next_ref[:, cols], 0.0)
            wk = [jnp.broadcast_to(cw_ref[k:k + 1, cols], (SUBLANES, LANES)) for k in range(CONV_W)]
            bias = jnp.broadcast_to(cb_ref[:, cols], (SUBLANES, LANES))
            for b in range(q // blk):
                r0 = SUBLANES + b * blk - CONV_W // 2
                taps = [xpad[j, pl.ds(r0 + u, SUBLANES, stride=st), :] for u in range(st + CONV_W - 1)]
                for v in range(st):
                    acc = bias
                    for k in range(CONV_W):
                        acc = acc + taps[v + k] * wk[k]
                    act[j, pl.ds(b * blk + v, SUBLANES, stride=st), :] = _silu(acc)
        n_x = D_INNER // LANES
        for g in range(SSM_GROUPS):
            c_keep[c, :, g * D_STATE:(g + 1) * D_STATE] = act[n_x + SSM_GROUPS + g].astype(BF16)

        dtr = dt_ref[...] + dtb_ref[...]
        dtv = jnp.maximum(dtr, 0.0) + jnp.log1p(jnp.exp(-jnp.abs(dtr)))
        a2 = dtv * (-LOG2E * jnp.exp(alog_ref[...]))
        ri = lax.broadcasted_iota(jnp.int32, (q, q), 0)
        ci = lax.broadcasted_iota(jnp.int32, (q, q), 1)
        lower = ci <= ri
        slower = ci < ri
        supper = ci > ri
        tot = jnp.sum(a2, axis=0, keepdims=True)
        dirs = []
        for d, keep in enumerate((lower, ci >= ri)):
            ad = a2[:, d * LANES:(d + 1) * LANES]
            dtd = dtv[:, d * LANES:(d + 1) * LANES]
            td = tot[:, d * LANES:(d + 1) * LANES]
            tri = keep.astype(BF16)
            cum = sum(jnp.dot(tri, part, preferred_element_type=F32) for part in _split3(ad))
            dirs.append(dict(cum=cum, cum_t=cum.T, dt_t=dtd.T, ecum=jnp.exp2(cum), etot=jnp.exp2(td),
                             wout_t=(jnp.exp2(td - cum) * dtd).T))
        fw, bw = dirs
        dt_sum_t = fw["dt_t"] + bw["dt_t"]
        ecb_keep[c] = bw["ecum"]
        etb_keep[c] = jnp.broadcast_to(bw["etot"], (SUBLANES, LANES))
        dsum = dskip_ref[0:1, :] + dskip_ref[1:2, :]

        for g in range(SSM_GROUPS):
            bm = act[n_x + g]
            cm_b = c_keep[c, :, g * D_STATE:(g + 1) * D_STATE]
            bm_t = bm.T
            cb = lax.dot_general(cm_b, bm.astype(BF16), NT, preferred_element_type=F32)
            for pp in range(hpg // 2):
                p = g * (hpg // 2) + pp
                xp = act[p]
                xbd = jnp.concatenate([jnp.where(left, xp, 0.0), jnp.where(left, 0.0, xp)], axis=0).astype(BF16)
                ms, bfs, bbs = [], [], []
                for h in (2 * p, 2 * p + 1):
                    seg = jnp.where(lower, fw["cum"][:, h:h + 1] - fw["cum_t"][h:h + 1, :],
                                    bw["cum"][:, h:h + 1] - bw["cum_t"][h:h + 1, :])
                    wgt = jnp.where(slower, fw["dt_t"][h:h + 1, :],
                                    jnp.where(supper, bw["dt_t"][h:h + 1, :], dt_sum_t[h:h + 1, :]))
                    ms.append((cb * jnp.exp2(seg) * wgt).astype(BF16))
                    bfs.append((bm_t * fw["wout_t"][h:h + 1, :]).astype(BF16))
                    bbs.append((bm_t * bw["wout_t"][h:h + 1, :]).astype(BF16))
                y = jnp.dot(jnp.concatenate(ms, axis=1), xbd, preferred_element_type=F32)
                st = s_f[p]
                y_off = jnp.dot(cm_b, st.astype(BF16), preferred_element_type=F32)
                y_keep[c, :, p * LANES:(p + 1) * LANES] = (y + y_off * pair_cols(fw["ecum"], p)
                                                        + dsum[:, p * LANES:(p + 1) * LANES] * xp)
                s_f[p] = st * pair_cols(fw["etot"], p) + jnp.dot(jnp.concatenate(bfs, axis=1), xbd,
                                                                 preferred_element_type=F32)
                sb_loc[c, p] = jnp.dot(jnp.concatenate(bbs, axis=1), xbd, preferred_element_type=F32)

    @pl.when(step >= nc)
    def _sweep_down():
        c = 2 * nc - 1 - step
        ecb = ecb_keep[c]
        etb = etb_keep[c][0:1]
        for p in range(npairs):
            g = p // (hpg // 2)
            st = s_b[p]
            y_off = jnp.dot(c_keep[c, :, g * D_STATE:(g + 1) * D_STATE], st.astype(BF16),
                            preferred_element_type=F32)
            y_ref[:, p * LANES:(p + 1) * LANES] = (y_keep[c, :, p * LANES:(p + 1) * LANES]
                                                    + y_off * pair_cols(ecb, p))
            s_b[p] = st * pair_cols(etb, p) + sb_loc[c, p]

        if emit_state:
            @pl.when(step == 2 * nc - 1)
            def _emit():
                for l in range(layer):
                    st_ref[l] = prev_st_ref[l]
                for p in range(npairs):
                    st_ref[layer, 0, p] = s_f[p].T
                    st_ref[layer, 1, p] = s_b[p].T


def _ssd_scan(xbc, dt, grp, conv_w, conv_b, dt_bias, a_log, d_skip, h0, state_out):
    t = xbc.shape[0]
    q = CHUNK
    nc = grp.length // q
    zero_init = h0 is None
    emit_state = state_out is not None
    boff = grp.row0 // q
    npairs = SSM_HEADS // 2
    qs = q // SUBLANES

    chunk = lambda b, s: boff + b * nc + jnp.minimum(s, nc - 1)
    in_specs = [
        pl.BlockSpec((q, CONV_DIM), lambda b, s: (chunk(b, s), 0)),
        pl.BlockSpec((SUBLANES, CONV_DIM), lambda b, s: (jnp.maximum(chunk(b, s) * qs - 1, 0), 0)),
        pl.BlockSpec((SUBLANES, CONV_DIM), lambda b, s: (jnp.minimum((chunk(b, s) + 1) * qs, t // SUBLANES - 1), 0)),
        pl.BlockSpec((q, 2 * LANES), lambda b, s: (chunk(b, s), 0)),
        _const_spec((SUBLANES, CONV_DIM)), _const_spec((1, CONV_DIM)),
        _const_spec((1, 2 * LANES)), _const_spec((1, 2 * LANES)), _const_spec((2, D_INNER)),
    ]
    args = [xbc, xbc, xbc, dt, conv_w, conv_b, dt_bias, a_log, d_skip]
    if not zero_init:
        in_specs.append(pl.BlockSpec((None, 2, npairs, D_STATE, LANES), lambda b, s: (b, 0, 0, 0, 0)))
        args.append(h0)
    state_block = lambda n: pl.BlockSpec((None, n, 2, npairs, LANES, D_STATE), lambda b, s: (b, 0, 0, 0, 0, 0))
    layer = state_out[1] if emit_state else 0
    if layer:
        in_specs.append(state_block(layer))
        args.append(state_out[0])
    out_specs = [
        pl.BlockSpec((q, D_INNER), lambda b, s: (b * nc + jnp.where(s < nc, nc - 1, 2 * nc - 1 - s), 0)),
    ]
    out_shape = [jax.ShapeDtypeStruct((grp.rows, D_INNER), F32)]
    if emit_state:
        out_specs.append(state_block(layer + 1))
        out_shape.append(jax.ShapeDtypeStruct((grp.nb, layer + 1, 2, npairs, LANES, D_STATE), F32))
    return pl.pallas_call(
        functools.partial(_ssd_kernel, zero_init=zero_init, emit_state=emit_state, layer=layer, nc=nc),
        grid=(grp.nb, 2 * nc),
        in_specs=in_specs,
        out_specs=out_specs,
        out_shape=out_shape,
        scratch_shapes=[pltpu.VMEM((npairs, D_STATE, LANES), F32),
                        pltpu.VMEM((npairs, D_STATE, LANES), F32),
                        pltpu.VMEM((nc, npairs, D_STATE, LANES), F32),
                        pltpu.VMEM((nc, q, D_INNER), F32),
                        pltpu.VMEM((nc, q, SSM_GROUPS * D_STATE), BF16),
                        pltpu.VMEM((nc, q, LANES), F32),
                        pltpu.VMEM((nc, SUBLANES, LANES), F32),
                        pltpu.VMEM((CONV_DIM // LANES, q + 2 * SUBLANES, LANES), F32),
                        pltpu.VMEM((CONV_DIM // LANES, q, LANES), F32)],
        compiler_params=_cparams(("arbitrary", "arbitrary")),
        name="ssd_scan",
    )(*args)


def _ssm_out_kernel(yp_ref, ys_ref, z_ref, ng_ref, w_ref, x_ref, mod_ref, o_ref, *, n_prompt_tiles):
    y = jnp.where(pl.program_id(0) < n_prompt_tiles, yp_ref[...], ys_ref[...])
    y = y * _silu(z_ref[...].astype(F32))
    yn = _rmsnorm(y, ng_ref[...]).astype(BF16)
    o_ref[...] = x_ref[...] + mod_ref[...][2:3] * jnp.dot(yn, w_ref[...], preferred_element_type=F32)


def _pair_state(h):
    b = h.shape[0]
    return h.reshape(b, 2, SSM_HEADS // 2, 2, SSM_HEADDIM, D_STATE).transpose(0, 1, 2, 5, 3, 4).reshape(
        b, 2, SSM_HEADS // 2, D_STATE, 2 * SSM_HEADDIM)


def _ssm_layer(x, modt, groups, layer, norm_g3, h0_sample, state_out, weights, conv_w, conv_b,
               dt_bias, a_log, d_skip, norm_g):
    j, w_z, w_x, w_dt, w_out = weights
    t = x.shape[0]
    tm = TOKEN_TILE
    slot = 3 * layer + 1
    row = lambda c: pl.BlockSpec((tm, c), lambda i: (i, 0))
    z, xbc, dt = pl.pallas_call(
        _ssm_proj_kernel,
        grid=(t // tm,),
        in_specs=[row(D_MODEL), _mod_spec(groups, tm, slot), _const_spec((1, D_MODEL), (slot,)),
                  _const_spec(w_z.shape[1:], (j,)), _const_spec(w_x.shape[1:], (j,)),
                  _const_spec(w_dt.shape[1:], (j,))],
        out_specs=[row(D_INNER), row(CONV_DIM), row(2 * LANES)],
        out_shape=[jax.ShapeDtypeStruct((t, D_INNER), BF16), jax.ShapeDtypeStruct((t, CONV_DIM), F32),
                   jax.ShapeDtypeStruct((t, 2 * LANES), F32)],
        compiler_params=_cparams(("arbitrary",)),
        name="ssm_project",
    )(x, modt, norm_g3, w_z, w_x, w_dt)

    lane_pad = lambda v: jnp.pad(v, ((0, 0), (0, LANES - v.shape[-1]))).reshape(1, 2 * LANES)
    cw = jnp.pad(conv_w, ((0, SUBLANES - CONV_W), (0, 0)))
    cb = conv_b.reshape(1, CONV_DIM)
    dsk = jnp.repeat(d_skip, SSM_HEADDIM, axis=-1)
    prompt, sample = groups
    scan = functools.partial(_ssd_scan, xbc, dt)
    yp, new_state = scan(prompt, cw, cb, lane_pad(dt_bias), lane_pad(a_log), dsk, None, state_out)
    ys, = scan(sample, cw, cb, lane_pad(dt_bias), lane_pad(a_log), dsk, _pair_state(h0_sample), None)

    to = TOKEN_TILE
    row = lambda c: pl.BlockSpec((to, c), lambda i: (i, 0))
    y_specs, n_p = _two_group_specs(groups, to, D_INNER)
    x = pl.pallas_call(
        functools.partial(_ssm_out_kernel, n_prompt_tiles=n_p),
        grid=(t // to,),
        in_specs=y_specs + [row(D_INNER), _const_spec((1, D_INNER)),
                  _const_spec((D_INNER, D_MODEL), (j,)), row(D_MODEL), _mod_spec(groups, to, slot)],
        out_specs=row(D_MODEL),
        out_shape=jax.ShapeDtypeStruct((t, D_MODEL), F32),
        compiler_params=_cparams(("arbitrary",)),
        name="ssm_out",
    )(yp, ys, z, norm_g.reshape(1, D_INNER), w_out, x, modt)
    return x, new_state


def kernel(x_prompt, x_sample, cache_mla, state_ssm, c, c_ctx, mod_w, mod_b, norm_g, ffn_w_in, ffn_w_out,
           mla_w_in, mla_q_norm, mla_kv_norm, mla_wq_b, mla_wkv_b, mla_wo, ssm_w_in, ssm_conv_w, ssm_conv_b,
           ssm_dt_bias, ssm_a_log, ssm_d, ssm_norm_g, ssm_w_out, final_norm_g):
    nbp, lp, d = x_prompt.shape
    nbs, ls, _ = x_sample.shape
    prompt = _Group(0, nbp, lp)
    sample = _Group(nbp * lp, nbs, ls)
    groups = (prompt, sample)
    x = jnp.concatenate([x_prompt.reshape(-1, d), x_sample.reshape(-1, d)], axis=0)

    ncond = 1 + nbs
    sc = jnp.concatenate([c_ctx[None, :], c, jnp.zeros((SUBLANES - ncond, d), F32)], axis=0)
    mod = _modulation(sc, mod_w, mod_b)
    modt = mod[:, :ncond].reshape(DEPTH, ncond, 3, 3, d).transpose(1, 0, 2, 3, 4).reshape(ncond, DEPTH * 3, 3, d)

    norm_g3 = norm_g.reshape(DEPTH * 3, 1, d)
    ffn_in = ffn_w_in
    ffn_out = ffn_w_out
    n_ssm = ssm_w_in.shape[0]
    ssm_wz = ssm_w_in[:, :, :D_INNER].astype(BF16)
    ssm_wx = ssm_w_in[:, :, D_INNER:D_INNER + CONV_DIM].astype(BF16)
    wdt = ssm_w_in[:, :, D_INNER + CONV_DIM:].reshape(n_ssm, d, 2, SSM_HEADS)
    ssm_wdt = jnp.pad(wdt, ((0, 0), (0, 0), (0, 0), (0, LANES - SSM_HEADS))).reshape(n_ssm, d, 2 * LANES).astype(BF16)
    ssm_wo = ssm_w_out.astype(BF16)

    cos, sin = _rope_tables(groups)
    new_mla = []
    new_state = None
    tiles_p = prompt.rows // FFN_TILE
    tiles_s = sample.rows // FFN_TILE
    for i in range(DEPTH):
        j = i // 2
        x = _ffn_half(x, modt, groups, i, 0, norm_g3, ffn_in, ffn_out, final_norm_g)
        if i % 2 == 0:
            x, entry = _mla_layer(x, modt, groups, i, norm_g3, cache_mla[:, j], cos, sin,
                                  mla_w_in[j], mla_q_norm[j], mla_kv_norm[j], mla_wq_b[j], mla_wkv_b[j], mla_wo[j])
            new_mla.append(entry)
        else:
            x, new_state = _ssm_layer(x, modt, groups, i, norm_g3, state_ssm[:, j], (new_state, j),
                                      (j, ssm_wz, ssm_wx, ssm_wdt, ssm_wo), ssm_conv_w[j], ssm_conv_b[j],
                                      ssm_dt_bias[j], ssm_a_log[j], ssm_d[j], ssm_norm_g[j])
        if i < DEPTH - 1:
            x = _ffn_half(x, modt, groups, i, 1, norm_g3, ffn_in, ffn_out, final_norm_g)
    last = functools.partial(_ffn_half, x, modt, groups, DEPTH - 1, 1, norm_g3, ffn_in, ffn_out, final_norm_g, True)
    y_prompt = last((0, tiles_p)).reshape(nbp, lp, d)
    y_sample = last((tiles_p, tiles_s)).reshape(nbs, ls, d)
    new_state = new_state.reshape(nbp, n_ssm, 2, SSM_HEADS, SSM_HEADDIM, D_STATE)
    return y_prompt, y_sample, jnp.stack(new_mla, axis=1), new_state
```

```python
import functools
import math

import jax
import jax.numpy as jnp
from jax import lax
from jax.experimental import pallas as pl
from jax.experimental.pallas import tpu as pltpu

F32 = jnp.float32
BF16 = jnp.bfloat16

D_MODEL = 1024
DEPTH = 4
N_MOD = 9
D_FF = 2816
GRID_W = 64
MLA_HEADS = 16
Q_LORA = 512
KV_LORA = 256
QK_NOPE = 64
QK_ROPE = 32
V_HEAD = 64
QK_HEAD = QK_NOPE + QK_ROPE
ROPE_BASE = 10000.0
HEAD_PAD = 128
VT_ROWS = 80
D_INNER = 2048
SSM_HEADDIM = 64
SSM_HEADS = 32
SSM_GROUPS = 4
D_STATE = 128
CONV_W = 5
CONV_DIM = D_INNER + 2 * SSM_GROUPS * D_STATE
CHUNK = 128
EPS = 1e-6
LOG2E = math.log2(math.e)

LANES = 128
SUBLANES = 8
TOKEN_TILE = 512
FFN_TILE = 512
FF_CHUNK = 256
FFN_STAGE_SLOTS = 4
VMEM_LIMIT = 56 * 1024 * 1024

NT = (((1,), (1,)), ((), ()))
TN = (((0,), (0,)), ((), ()))


def _cparams(sem):
    return pltpu.CompilerParams(dimension_semantics=sem, vmem_limit_bytes=VMEM_LIMIT)


def _const_spec(shape, index=None):
    index = tuple(index or ())
    block = (None,) * len(index) + tuple(shape)
    zeros = (0,) * len(shape)
    return pl.BlockSpec(block, lambda *_: index + zeros, pipeline_mode=pl.Buffered(1))


def _rmsnorm(x, g):
    ms = jnp.mean(x * x, axis=-1, keepdims=True)
    return (x * lax.rsqrt(ms + EPS)) * g


def _silu(x):
    h = 0.5 * x
    return h * jnp.tanh(h) + h


def _adaln(x, g, shift, scale):
    return _rmsnorm(x, g) * (1.0 + scale) + shift


class _Group:
    def __init__(self, row0, nb, length):
        self.row0, self.nb, self.length = row0, nb, length

    @property
    def rows(self):
        return self.nb * self.length


def _mod_kernel(sc_ref, w_ref, b_ref, o_ref):
    s = _silu(sc_ref[...])
    o_ref[...] = jnp.dot(s.astype(BF16), w_ref[...].astype(BF16),
                         preferred_element_type=F32) + b_ref[...]


def _modulation(sc, mod_w, mod_b):
    tn = 1024
    n = N_MOD * D_MODEL
    return pl.pallas_call(
        _mod_kernel,
        grid=(DEPTH, n // tn),
        in_specs=[
            pl.BlockSpec((SUBLANES, D_MODEL), lambda i, j: (0, 0)),
            pl.BlockSpec((None, D_MODEL, tn), lambda i, j: (i, 0, j)),
            pl.BlockSpec((None, 1, tn), lambda i, j: (i, 0, j)),
        ],
        out_specs=pl.BlockSpec((None, SUBLANES, tn), lambda i, j: (i, 0, j)),
        out_shape=jax.ShapeDtypeStruct((DEPTH, SUBLANES, n), F32),
        compiler_params=_cparams(("arbitrary", "arbitrary")),
        name="modulation",
    )(sc, mod_w, mod_b.reshape(DEPTH, 1, n))


def _mod_spec(groups, tile, slot, first_tile=0):
    bounds = []
    for g in groups:
        for b in range(g.nb if g.row0 else 1):
            bounds.append((g.row0 + b * g.length) // tile)

    def index_map(i, *_):
        gidx = 0
        for lo in bounds[1:]:
            gidx = gidx + (i + first_tile >= lo).astype(jnp.int32)
        return (gidx, slot, 0, 0)

    return pl.BlockSpec((None, None, 3, D_MODEL), index_map)


def _two_group_specs(groups, tile, width, axis=0):
    prompt, sample = groups
    n_p = prompt.rows // tile
    n_s = sample.rows // tile
    if axis == 0:
        return [pl.BlockSpec((tile, width), lambda i: (jnp.minimum(i, n_p - 1), 0)),
                pl.BlockSpec((tile, width), lambda i: (jnp.clip(i - n_p, 0, n_s - 1), 0))], n_p
    return [pl.BlockSpec((width, tile), lambda i: (0, jnp.minimum(i, n_p - 1))),
            pl.BlockSpec((width, tile), lambda i: (0, jnp.clip(i - n_p, 0, n_s - 1)))], n_p


def _ffn_kernel(*refs, layer, half, final_norm, n_prompt_tiles):
    if n_prompt_tiles is None:
        x_ref, refs = refs[0], refs[1:]
        load_x = lambda r0, r1: x_ref[r0:r1, :]
    else:
        xp_ref, xs_ref, refs = refs[0], refs[1], refs[2:]
        is_prompt = pl.program_id(0) < n_prompt_tiles
        load_x = lambda r0, r1: jnp.where(is_prompt, xp_ref[r0:r1, :], xs_ref[r0:r1, :])
    mod_ref, g_ref, fg_ref, win_hbm, wout_hbm, o_ref, win_b, wout_b, a_scr, stage_in, stage_out, sem = refs
    n_chunks = D_FF // FF_CHUNK
    n_slots = stage_out.shape[0]
    ahead = n_slots - 1

    def chunk_copies(c):
        lo = c * FF_CHUNK
        slot = c % n_slots
        w_in = win_hbm.at[layer, half]
        return (pltpu.make_async_copy(w_in.at[:, pl.ds(lo, FF_CHUNK)], stage_in.at[slot, 0], sem.at[slot, 0]),
                pltpu.make_async_copy(w_in.at[:, pl.ds(D_FF + lo, FF_CHUNK)], stage_in.at[slot, 1], sem.at[slot, 1]),
                pltpu.make_async_copy(wout_hbm.at[layer, half, pl.ds(lo, FF_CHUNK), :], stage_out.at[slot],
                                      sem.at[slot, 2]))

    m = mod_ref[...]

    def rows_pass(r0, r1, before_chunk=None):
        x = load_x(r0, r1)
        hb = _adaln(x, g_ref[...], m[0:1], m[1:2]).astype(BF16)
        for c in range(n_chunks):
            if before_chunk is not None:
                before_chunk(c)
            lo = c * FF_CHUNK
            gate = jnp.dot(hb, win_b[:, lo:lo + FF_CHUNK], preferred_element_type=F32)
            up = jnp.dot(hb, win_b[:, D_FF + lo:D_FF + lo + FF_CHUNK], preferred_element_type=F32)
            a_scr[r0:r1, lo:lo + FF_CHUNK] = (_silu(gate) * up).astype(BF16)
        y = jnp.dot(a_scr[r0:r1, :], wout_b[...], preferred_element_type=F32)
        out = x + (0.5 * m[2:3]) * y
        if final_norm:
            out = _rmsnorm(out, fg_ref[...])
        o_ref[r0:r1, :] = out

    def land_chunk(c):
        if c + ahead < n_chunks:
            for cp in chunk_copies(c + ahead):
                cp.start()
        for cp in chunk_copies(c):
            cp.wait()
        lo = c * FF_CHUNK
        slot = c % n_slots
        win_b[:, lo:lo + FF_CHUNK] = stage_in[slot, 0].astype(BF16)
        win_b[:, D_FF + lo:D_FF + lo + FF_CHUNK] = stage_in[slot, 1].astype(BF16)
        wout_b[lo:lo + FF_CHUNK, :] = stage_out[slot].astype(BF16)

    tm = o_ref.shape[0]

    @pl.when(pl.program_id(0) == 0)
    def _first_tile():
        for c in range(ahead):
            for cp in chunk_copies(c):
                cp.start()
        rows_pass(0, tm, land_chunk)

    @pl.when(pl.program_id(0) > 0)
    def _other_tiles():
        rows_pass(0, tm // 2)
        rows_pass(tm // 2, tm)


def _ffn_half(x, modt, groups, layer, half, norm_g3, w_in, w_out, final_g, final_norm=False, tiles=None):
    tm = FFN_TILE
    slot = 3 * layer + 2 * half
    if isinstance(x, tuple):
        x_specs, n_prompt_tiles = _two_group_specs(groups, tm, D_MODEL)
        first, count = 0, sum(g.rows for g in groups) // tm
    else:
        first, count = tiles or (0, x.shape[0] // tm)
        x_specs, n_prompt_tiles = [pl.BlockSpec((tm, D_MODEL), lambda i: (i + first, 0))], None
        x = (x,)
    return pl.pallas_call(
        functools.partial(_ffn_kernel, layer=layer, half=half, final_norm=final_norm,
                          n_prompt_tiles=n_prompt_tiles),
        grid=(count,),
        in_specs=x_specs + [
            _mod_spec(groups, tm, slot, first),
            _const_spec((1, D_MODEL), (slot,)),
            _const_spec((1, D_MODEL)),
            pl.BlockSpec(memory_space=pl.ANY),
            pl.BlockSpec(memory_space=pl.ANY),
        ],
        out_specs=pl.BlockSpec((tm, D_MODEL), lambda i: (i, 0)),
        out_shape=jax.ShapeDtypeStruct((count * tm, D_MODEL), F32),
        scratch_shapes=[pltpu.VMEM((D_MODEL, 2 * D_FF), BF16),
                        pltpu.VMEM((D_FF, D_MODEL), BF16),
                        pltpu.VMEM((tm, D_FF), BF16),
                        pltpu.VMEM((FFN_STAGE_SLOTS, 2, D_MODEL, FF_CHUNK), F32),
                        pltpu.VMEM((FFN_STAGE_SLOTS, FF_CHUNK, D_MODEL), F32),
                        pltpu.SemaphoreType.DMA((FFN_STAGE_SLOTS, 3))],
        compiler_params=_cparams(("arbitrary",)),
        name="ffn_half",
    )(*x, modt, norm_g3, final_g.reshape(1, D_MODEL), w_in, w_out)


def _expand_kv(ckv_b, krope, wk_ref, wvt_ref, k_ref, vt_ref):
    kn = jnp.dot(ckv_b, wk_ref[...], preferred_element_type=F32)
    for h in range(MLA_HEADS):
        lo = h * HEAD_PAD
        k_ref[:, lo:lo + HEAD_PAD] = (kn[:, lo:lo + HEAD_PAD] + krope).astype(BF16)
    vt_ref[...] = lax.dot_general(wvt_ref[...], ckv_b, NT, preferred_element_type=F32).astype(BF16)
    ones = jnp.ones((VT_ROWS - V_HEAD, ckv_b.shape[0]), BF16)
    for h in range(MLA_HEADS):
        vt_ref[h * VT_ROWS + V_HEAD:(h + 1) * VT_ROWS, :] = ones


def _mla_proj_kernel(x_ref, mod_ref, g_ref, win_ref, qn_ref, kvn_ref, wq1_ref,
                     wk_ref, wvt_ref, cos_ref, sin_ref,
                     q_ref, k_ref, vt_ref, ckv_ref, kr_ref):
    x = x_ref[...]
    m = mod_ref[...]
    hb = _adaln(x, g_ref[...], m[0:1], m[1:2]).astype(BF16)
    a = jnp.dot(hb, win_ref[...], preferred_element_type=F32)
    qa = _rmsnorm(a[:, :Q_LORA], qn_ref[...]).astype(BF16)
    ckv = _rmsnorm(a[:, Q_LORA:Q_LORA + KV_LORA], kvn_ref[...])
    kr = a[:, Q_LORA + KV_LORA:Q_LORA + KV_LORA + HEAD_PAD]
    kr_rot = a[:, Q_LORA + KV_LORA + HEAD_PAD:]
    cosk = cos_ref[...]
    sink = sin_ref[...]
    lane = lax.broadcasted_iota(jnp.int32, cosk.shape, 1)
    scale = LOG2E / math.sqrt(QK_HEAD)
    cosq = (cosk + (lane < QK_NOPE).astype(F32)) * scale
    q_rot = QK_ROPE // 4
    first = ((lane - QK_NOPE) % (2 * q_rot)) < q_rot
    sin_dn = jnp.where(first, -sink, 0.0) * scale
    sin_up = jnp.where(first, 0.0, sink) * scale
    q1 = jnp.dot(qa, wq1_ref[...], preferred_element_type=F32)
    for h in range(MLA_HEADS):
        lo = h * HEAD_PAD
        qh = q1[:, lo:lo + HEAD_PAD]
        q_ref[:, lo:lo + HEAD_PAD] = (qh * cosq + pltpu.roll(qh, HEAD_PAD - q_rot, 1) * sin_dn
                                      + pltpu.roll(qh, q_rot, 1) * sin_up).astype(BF16)
    ckv_ref[...] = ckv
    kr_ref[...] = kr
    _expand_kv(ckv.astype(BF16), kr * cosk + kr_rot * sink, wk_ref, wvt_ref, k_ref, vt_ref)


def _mla_ctx_kernel(ckv_ref, kr_ref, wk_ref, wvt_ref, k_ref, vt_ref):
    _expand_kv(ckv_ref[...].astype(BF16), kr_ref[...], wk_ref, wvt_ref, k_ref, vt_ref)


def _mla_attn_kernel(*refs, n_parts, heads):
    q_ref = refs[0]
    kv_refs = refs[1:1 + 2 * n_parts]
    o_ref = refs[1 + 2 * n_parts]
    def scores(h):
        qh = q_ref[:, h * HEAD_PAD:(h + 1) * HEAD_PAD]
        return [lax.dot_general(kv_refs[2 * i][:, h * HEAD_PAD:(h + 1) * HEAD_PAD], qh, NT,
                                preferred_element_type=F32) for i in range(n_parts)]

    ss_next = scores(0)
    for h in range(heads):
        ss = ss_next
        if h + 1 < heads:
            ss_next = scores(h + 1)
        mx = ss[0].max(axis=0, keepdims=True)
        for s in ss[1:]:
            mx = jnp.maximum(mx, s.max(axis=0, keepdims=True))
        acc = None
        for i, s in enumerate(ss):
            p = jnp.exp2(s - mx).astype(BF16)
            o = jnp.dot(kv_refs[2 * i + 1][h * VT_ROWS:(h + 1) * VT_ROWS, :], p, preferred_element_type=F32)
            acc = o if acc is None else acc + o
        o_ref[h * V_HEAD:(h + 1) * V_HEAD, :] = (acc[:V_HEAD] / acc[V_HEAD:V_HEAD + 1]).astype(BF16)


def _mla_attention(q, parts, grp, tq, heads):
    nq = grp.length // tq
    qoff = grp.row0 // tq
    in_specs = [pl.BlockSpec((tq, heads * HEAD_PAD), lambda b, p, i: (qoff + b * nq + i, p))]
    args = [q]
    for k, vt, row0, lk in parts:
        koff = row0 // lk
        in_specs.append(pl.BlockSpec((lk, heads * HEAD_PAD), lambda b, p, i, koff=koff: (koff + b, p)))
        in_specs.append(pl.BlockSpec((heads * VT_ROWS, lk), lambda b, p, i, koff=koff: (p, koff + b)))
        args += [k, vt]
    return pl.pallas_call(
        functools.partial(_mla_attn_kernel, n_parts=len(parts), heads=heads),
        grid=(grp.nb, MLA_HEADS // heads, nq),
        in_specs=in_specs,
        out_specs=pl.BlockSpec((heads * V_HEAD, tq), lambda b, p, i: (p, b * nq + i)),
        out_shape=jax.ShapeDtypeStruct((MLA_HEADS * V_HEAD, grp.rows), BF16),
        compiler_params=_cparams(("arbitrary", "arbitrary", "arbitrary")),
        name="mla_attention",
    )(*args)


def _mla_out_kernel(otp_ref, ots_ref, wo_ref, x_ref, mod_ref, o_ref, *, n_prompt_tiles):
    o_t = jnp.where(pl.program_id(0) < n_prompt_tiles, otp_ref[...], ots_ref[...])
    y = lax.dot_general(o_t, wo_ref[...], TN, preferred_element_type=F32)
    o_ref[...] = x_ref[...] + mod_ref[...][2:3] * y


def _rope_tables(groups):
    prompt, sample = groups
    half = QK_ROPE // 2
    freqs = 1.0 / (ROPE_BASE ** (jnp.arange(0, half, 2, dtype=F32) / half))
    pos_i = jnp.arange(sample.length)
    pos = jnp.stack([pos_i // GRID_W, pos_i % GRID_W], axis=-1).astype(F32)
    ang = pos[:, :, None] * freqs
    ang = jnp.broadcast_to(ang[:, :, None, :], (sample.length, 2, 2, half // 2)).reshape(sample.length, QK_ROPE)
    lanes = lambda v: jnp.pad(v, ((0, 0), (QK_NOPE, HEAD_PAD - QK_HEAD)))
    cos = jnp.concatenate([lanes(jnp.ones((prompt.rows, QK_ROPE), F32))] + [lanes(jnp.cos(ang))] * sample.nb, axis=0)
    sin = jnp.concatenate([jnp.zeros((prompt.rows, HEAD_PAD), F32)] + [lanes(jnp.sin(ang))] * sample.nb, axis=0)
    return cos, sin


def _rot_cols(w):
    q = QK_ROPE // 4
    w4 = w.reshape(w.shape[:-1] + (2, 2, q))
    return jnp.stack([-w4[..., 1, :], w4[..., 0, :]], axis=-2).reshape(w.shape)


def _mla_weights(w_in, wq_b, wkv_b):
    d = w_in.shape[0]
    z = lambda *s: jnp.zeros(s, F32)
    k_r = w_in[:, Q_LORA + KV_LORA:]
    pad_tile = lambda c: jnp.concatenate([z(d, QK_NOPE), c, z(d, HEAD_PAD - QK_HEAD)], axis=1)
    w_in_ext = jnp.concatenate([w_in[:, :Q_LORA + KV_LORA], pad_tile(k_r), pad_tile(_rot_cols(k_r))], axis=1)
    wq = wq_b.reshape(Q_LORA, MLA_HEADS, QK_HEAD)
    zq = lambda n: z(Q_LORA, MLA_HEADS, n)
    wq1 = jnp.concatenate([wq, zq(HEAD_PAD - QK_HEAD)], axis=-1)
    wkv = wkv_b.reshape(KV_LORA, MLA_HEADS, QK_NOPE + V_HEAD)
    wk = jnp.concatenate([wkv[..., :QK_NOPE], z(KV_LORA, MLA_HEADS, HEAD_PAD - QK_NOPE)], axis=-1)
    wv = jnp.concatenate([wkv[..., QK_NOPE:], z(KV_LORA, MLA_HEADS, VT_ROWS - V_HEAD)], axis=-1)
    wvt = wv.reshape(KV_LORA, MLA_HEADS * VT_ROWS).T
    flat = lambda w: w.reshape(w.shape[0], -1).astype(BF16)
    return w_in_ext.astype(BF16), flat(wq1), flat(wk), wvt.astype(BF16)


def _mla_layer(x, modt, groups, layer, norm_g3, cache, cos, sin, w_in, q_norm, kv_norm, wq_b, wkv_b, wo):
    t = x.shape[0]
    tm = TOKEN_TILE
    slot = 3 * layer + 1
    w_in_ext, wq1, wk, wvt = _mla_weights(w_in, wq_b, wkv_b)
    hd = MLA_HEADS * HEAD_PAD
    vr = MLA_HEADS * VT_ROWS
    row = lambda c: pl.BlockSpec((tm, c), lambda i: (i, 0))
    q, k, vt, ckv, kr = pl.pallas_call(
        _mla_proj_kernel,
        grid=(t // tm,),
        in_specs=[
            row(D_MODEL), _mod_spec(groups, tm, slot), _const_spec((1, D_MODEL), (slot,)),
            _const_spec(w_in_ext.shape), _const_spec((1, Q_LORA)), _const_spec((1, KV_LORA)),
            _const_spec(wq1.shape), _const_spec(wk.shape), _const_spec(wvt.shape),
            row(HEAD_PAD), row(HEAD_PAD),
        ],
        out_specs=[row(hd), row(hd), pl.BlockSpec((vr, tm), lambda i: (0, i)), row(KV_LORA), row(HEAD_PAD)],
        out_shape=[jax.ShapeDtypeStruct((t, hd), BF16), jax.ShapeDtypeStruct((t, hd), BF16),
                   jax.ShapeDtypeStruct((vr, t), BF16), jax.ShapeDtypeStruct((t, KV_LORA), F32),
                   jax.ShapeDtypeStruct((t, HEAD_PAD), F32)],
        compiler_params=_cparams(("arbitrary",)),
        name="mla_project",
    )(x, modt, norm_g3, w_in_ext, q_norm.reshape(1, -1), kv_norm.reshape(1, -1),
      wq1, wk, wvt, cos, sin)

    prompt, sample = groups
    nb, past, _ = cache.shape
    cflat = cache.reshape(nb * past, -1)
    ckv_c = cflat[:, :KV_LORA]
    kr_c = jnp.pad(cflat[:, KV_LORA:], ((0, 0), (QK_NOPE, HEAD_PAD - QK_HEAD)))
    crow = lambda c: pl.BlockSpec((past, c), lambda i: (i, 0))
    k_c, vt_c = pl.pallas_call(
        _mla_ctx_kernel,
        grid=(nb,),
        in_specs=[crow(KV_LORA), crow(HEAD_PAD), _const_spec(wk.shape), _const_spec(wvt.shape)],
        out_specs=[crow(hd), pl.BlockSpec((vr, past), lambda i: (0, i))],
        out_shape=[jax.ShapeDtypeStruct((nb * past, hd), BF16), jax.ShapeDtypeStruct((vr, nb * past), BF16)],
        compiler_params=_cparams(("arbitrary",)),
        name="mla_ctx_expand",
    )(ckv_c, kr_c, wk, wvt)

    ot_p = _mla_attention(q, [(k, vt, prompt.row0, prompt.length)], prompt, 256, MLA_HEADS)
    ot_s = _mla_attention(q, [(k_c, vt_c, 0, past), (k, vt, sample.row0, sample.length)], sample, 512, 8)
    vd = MLA_HEADS * V_HEAD
    o_specs, n_p = _two_group_specs(groups, tm, vd, axis=1)
    x = pl.pallas_call(
        functools.partial(_mla_out_kernel, n_prompt_tiles=n_p),
        grid=(t // tm,),
        in_specs=o_specs + [_const_spec((vd, D_MODEL)), row(D_MODEL), _mod_spec(groups, tm, slot)],
        out_specs=row(D_MODEL),
        out_shape=jax.ShapeDtypeStruct((t, D_MODEL), F32),
        compiler_params=_cparams(("arbitrary",)),
        name="mla_out",
    )(ot_p, ot_s, wo.astype(BF16), x, modt)
    entry = jnp.concatenate([ckv[:prompt.rows], kr[:prompt.rows, QK_NOPE:QK_HEAD]], axis=-1)
    return x, entry.reshape(prompt.nb, prompt.length, -1)


def _ssm_proj_kernel(x_ref, mod_ref, g_ref, wz_ref, wx_ref, wdt_ref, z_ref, xbc_ref, dt_ref):
    m = mod_ref[...]
    hb = _adaln(x_ref[...], g_ref[...], m[0:1], m[1:2]).astype(BF16)
    z_ref[...] = jnp.dot(hb, wz_ref[...], preferred_element_type=F32).astype(BF16)
    xbc_ref[...] = jnp.dot(hb, wx_ref[...], preferred_element_type=F32)
    dt_ref[...] = jnp.dot(hb, wdt_ref[...], preferred_element_type=F32)


def _split3(a):
    hi = a.astype(BF16)
    r1 = a - hi.astype(F32)
    mid = r1.astype(BF16)
    lo = (r1 - mid.astype(F32)).astype(BF16)
    return hi, mid, lo


def _ssd_kernel(*refs, zero_init, emit_state, layer, nc):
    it = iter(refs)
    cur_ref, prev_ref, next_ref, dt_ref = next(it), next(it), next(it), next(it)
    cw_ref, cb_ref, dtb_ref, alog_ref, dskip_ref = next(it), next(it), next(it), next(it), next(it)
    h0_ref = None if zero_init else next(it)
    prev_st_ref = next(it) if layer else None
    y_ref = next(it)
    st_ref = next(it) if emit_state else None
    s_f, s_b, sb_loc, y_keep, c_keep, ecb_keep, etb_keep, xpad, act = (next(it) for _ in range(9))

    q = CHUNK
    step = pl.program_id(1)
    npairs = SSM_HEADS // 2
    hpg = SSM_HEADS // SSM_GROUPS
    lane = lax.broadcasted_iota(jnp.int32, (q, LANES), 1)
    left = lane < SSM_HEADDIM

    def pair_cols(v, p):
        return jnp.where(left[:v.shape[0]], v[:, 2 * p:2 * p + 1], v[:, 2 * p + 1:2 * p + 2])

    @pl.when(step == 0)
    def _init():
        if zero_init:
            s_f[...] = jnp.zeros(s_f.shape, F32)
            s_b[...] = jnp.zeros(s_b.shape, F32)
        else:
            s_f[...] = h0_ref[0]
            s_b[...] = h0_ref[1]

    @pl.when(step < nc)
    def _sweep_up():
        c = step
        blk = 4 * SUBLANES
        st = blk // SUBLANES
        has_prev = c > 0
        has_next = c < nc - 1
        for j in range(CONV_DIM // LANES):
            cols = slice(j * LANES, (j + 1) * LANES)
            xpad[j, 0:SUBLANES, :] = jnp.where(has_prev, prev_ref[:, cols], 0.0)
            xpad[j, SUBLANES:SUBLANES + q, :] = cur_ref[:, cols]
            xpad[j, SUBLANES + q:, :] = jnp.where(has_next, next_ref[:, cols], 0.0)
            wk = [jnp.broadcast_to(cw_ref[k:k + 1, cols], (SUBLANES, LANES)) for k in range(CONV_W)]
            bias = jnp.broadcast_to(cb_ref[:, cols], (SUBLANES, LANES))
            for b in range(q // blk):
                r0 = SUBLANES + b * blk - CONV_W // 2
                taps = [xpad[j, pl.ds(r0 + u, SUBLANES, stride=st), :] for u in range(st + CONV_W - 1)]
                for v in range(st):
                    acc = bias
                    for k in range(CONV_W):
                        acc = acc + taps[v + k] * wk[k]
                    act[j, pl.ds(b * blk + v, SUBLANES, stride=st), :] = _silu(acc)
        n_x = D_INNER // LANES
        for g in range(SSM_GROUPS):
            c_keep[c, :, g * D_STATE:(g + 1) * D_STATE] = act[n_x + SSM_GROUPS + g].astype(BF16)

        dtr = dt_ref[...] + dtb_ref[...]
        dtv = jnp.maximum(dtr, 0.0) + jnp.log1p(jnp.exp(-jnp.abs(dtr)))
        a2 = dtv * (-LOG2E * jnp.exp(alog_ref[...]))
        ri = lax.broadcasted_iota(jnp.int32, (q, q), 0)
        ci = lax.broadcasted_iota(jnp.int32, (q, q), 1)
        lower = ci <= ri
        slower = ci < ri
        supper = ci > ri
        tot = jnp.sum(a2, axis=0, keepdims=True)
        dirs = []
        for d, keep in enumerate((lower, ci >= ri)):
            ad = a2[:, d * LANES:(d + 1) * LANES]
            dtd = dtv[:, d * LANES:(d + 1) * LANES]
            td = tot[:, d * LANES:(d + 1) * LANES]
            tri = keep.astype(BF16)
            cum = sum(jnp.dot(tri, part, preferred_element_type=F32) for part in _split3(ad))
            dirs.append(dict(cum=cum, cum_t=cum.T, dt_t=dtd.T, ecum=jnp.exp2(cum), etot=jnp.exp2(td),
                             wout_t=(jnp.exp2(td - cum) * dtd).T))
        fw, bw = dirs
        dt_sum_t = fw["dt_t"] + bw["dt_t"]
        ecb_keep[c] = bw["ecum"]
        etb_keep[c] = jnp.broadcast_to(bw["etot"], (SUBLANES, LANES))
        dsum = dskip_ref[0:1, :] + dskip_ref[1:2, :]

        for g in range(SSM_GROUPS):
            bm = act[n_x + g]
            cm_b = c_keep[c, :, g * D_STATE:(g + 1) * D_STATE]
            bm_t = bm.T
            cb = lax.dot_general(cm_b, bm.astype(BF16), NT, preferred_element_type=F32)
            for pp in range(hpg // 2):
                p = g * (hpg // 2) + pp
                xp = act[p]
                xbd = jnp.concatenate([jnp.where(left, xp, 0.0), jnp.where(left, 0.0, xp)], axis=0).astype(BF16)
                ms, bfs, bbs = [], [], []
                for h in (2 * p, 2 * p + 1):
                    seg = jnp.where(lower, fw["cum"][:, h:h + 1] - fw["cum_t"][h:h + 1, :],
                                    bw["cum"][:, h:h + 1] - bw["cum_t"][h:h + 1, :])
                    wgt = jnp.where(slower, fw["dt_t"][h:h + 1, :],
                                    jnp.where(supper, bw["dt_t"][h:h + 1, :], dt_sum_t[h:h + 1, :]))
                    ms.append((cb * jnp.exp2(seg) * wgt).astype(BF16))
                    bfs.append((bm_t * fw["wout_t"][h:h + 1, :]).astype(BF16))
                    bbs.append((bm_t * bw["wout_t"][h:h + 1, :]).astype(BF16))
                y = jnp.dot(jnp.concatenate(ms, axis=1), xbd, preferred_element_type=F32)
                st = s_f[p]
                y_off = jnp.dot(cm_b, st.astype(BF16), preferred_element_type=F32)
                y_keep[c, :, p * LANES:(p + 1) * LANES] = (y + y_off * pair_cols(fw["ecum"], p)
                                                        + dsum[:, p * LANES:(p + 1) * LANES] * xp)
                s_f[p] = st * pair_cols(fw["etot"], p) + jnp.dot(jnp.concatenate(bfs, axis=1), xbd,
                                                                 preferred_element_type=F32)
                sb_loc[c, p] = jnp.dot(jnp.concatenate(bbs, axis=1), xbd, preferred_element_type=F32)

    @pl.when(step >= nc)
    def _sweep_down():
        c = 2 * nc - 1 - step
        ecb = ecb_keep[c]
        etb = etb_keep[c][0:1]
        for p in range(npairs):
            g = p // (hpg // 2)
            st = s_b[p]
            y_off = jnp.dot(c_keep[c, :, g * D_STATE:(g + 1) * D_STATE], st.astype(BF16),
                            preferred_element_type=F32)
            y_ref[:, p * LANES:(p + 1) * LANES] = (y_keep[c, :, p * LANES:(p + 1) * LANES]
                                                    + y_off * pair_cols(ecb, p))
            s_b[p] = st * pair_cols(etb, p) + sb_loc[c, p]

        if emit_state:
            @pl.when(step == 2 * nc - 1)
            def _emit():
                for l in range(layer):
                    st_ref[l] = prev_st_ref[l]
                for p in range(npairs):
                    st_ref[layer, 0, p] = s_f[p].T
                    st_ref[layer, 1, p] = s_b[p].T


def _ssd_scan(xbc, dt, grp, conv_w, conv_b, dt_bias, a_log, d_skip, h0, state_out):
    t = xbc.shape[0]
    q = CHUNK
    nc = grp.length // q
    zero_init = h0 is None
    emit_state = state_out is not None
    boff = grp.row0 // q
    npairs = SSM_HEADS // 2
    qs = q // SUBLANES

    chunk = lambda b, s: boff + b * nc + jnp.minimum(s, nc - 1)
    in_specs = [
        pl.BlockSpec((q, CONV_DIM), lambda b, s: (chunk(b, s), 0)),
        pl.BlockSpec((SUBLANES, CONV_DIM), lambda b, s: (jnp.maximum(chunk(b, s) * qs - 1, 0), 0)),
        pl.BlockSpec((SUBLANES, CONV_DIM), lambda b, s: (jnp.minimum((chunk(b, s) + 1) * qs, t // SUBLANES - 1), 0)),
        pl.BlockSpec((q, 2 * LANES), lambda b, s: (chunk(b, s), 0)),
        _const_spec((SUBLANES, CONV_DIM)), _const_spec((1, CONV_DIM)),
        _const_spec((1, 2 * LANES)), _const_spec((1, 2 * LANES)), _const_spec((2, D_INNER)),
    ]
    args = [xbc, xbc, xbc, dt, conv_w, conv_b, dt_bias, a_log, d_skip]
    if not zero_init:
        in_specs.append(pl.BlockSpec((None, 2, npairs, D_STATE, LANES), lambda b, s: (b, 0, 0, 0, 0)))
        args.append(h0)
    state_block = lambda n: pl.BlockSpec((None, n, 2, npairs, LANES, D_STATE), lambda b, s: (b, 0, 0, 0, 0, 0))
    layer = state_out[1] if emit_state else 0
    if layer:
        in_specs.append(state_block(layer))
        args.append(state_out[0])
    out_specs = [
        pl.BlockSpec((q, D_INNER), lambda b, s: (b * nc + jnp.where(s < nc, nc - 1, 2 * nc - 1 - s), 0)),
    ]
    out_shape = [jax.ShapeDtypeStruct((grp.rows, D_INNER), F32)]
    if emit_state:
        out_specs.append(state_block(layer + 1))
        out_shape.append(jax.ShapeDtypeStruct((grp.nb, layer + 1, 2, npairs, LANES, D_STATE), F32))
    return pl.pallas_call(
        functools.partial(_ssd_kernel, zero_init=zero_init, emit_state=emit_state, layer=layer, nc=nc),
        grid=(grp.nb, 2 * nc),
        in_specs=in_specs,
        out_specs=out_specs,
        out_shape=out_shape,
        scratch_shapes=[pltpu.VMEM((npairs, D_STATE, LANES), F32),
                        pltpu.VMEM((npairs, D_STATE, LANES), F32),
                        pltpu.VMEM((nc, npairs, D_STATE, LANES), F32),
                        pltpu.VMEM((nc, q, D_INNER), F32),
                        pltpu.VMEM((nc, q, SSM_GROUPS * D_STATE), BF16),
                        pltpu.VMEM((nc, q, LANES), F32),
                        pltpu.VMEM((nc, SUBLANES, LANES), F32),
                        pltpu.VMEM((CONV_DIM // LANES, q + 2 * SUBLANES, LANES), F32),
                        pltpu.VMEM((CONV_DIM // LANES, q, LANES), F32)],
        compiler_params=_cparams(("arbitrary", "arbitrary")),
        name="ssd_scan",
    )(*args)


def _ssm_out_kernel(yp_ref, ys_ref, z_ref, ng_ref, w_ref, x_ref, mod_ref, o_ref, *, n_prompt_tiles):
    y = jnp.where(pl.program_id(0) < n_prompt_tiles, yp_ref[...], ys_ref[...])
    y = y * _silu(z_ref[...].astype(F32))
    yn = _rmsnorm(y, ng_ref[...]).astype(BF16)
    o_ref[...] = x_ref[...] + mod_ref[...][2:3] * jnp.dot(yn, w_ref[...], preferred_element_type=F32)


def _pair_state(h):
    b = h.shape[0]
    return h.reshape(b, 2, SSM_HEADS // 2, 2, SSM_HEADDIM, D_STATE).transpose(0, 1, 2, 5, 3, 4).reshape(
        b, 2, SSM_HEADS // 2, D_STATE, 2 * SSM_HEADDIM)


def _ssm_layer(x, modt, groups, layer, norm_g3, h0_sample, state_out, weights, conv_w, conv_b,
               dt_bias, a_log, d_skip, norm_g):
    j, w_z, w_x, w_dt, w_out = weights
    t = x.shape[0]
    tm = TOKEN_TILE
    slot = 3 * layer + 1
    row = lambda c: pl.BlockSpec((tm, c), lambda i: (i, 0))
    z, xbc, dt = pl.pallas_call(
        _ssm_proj_kernel,
        grid=(t // tm,),
        in_specs=[row(D_MODEL), _mod_spec(groups, tm, slot), _const_spec((1, D_MODEL), (slot,)),
                  _const_spec(w_z.shape[1:], (j,)), _const_spec(w_x.shape[1:], (j,)),
                  _const_spec(w_dt.shape[1:], (j,))],
        out_specs=[row(D_INNER), row(CONV_DIM), row(2 * LANES)],
        out_shape=[jax.ShapeDtypeStruct((t, D_INNER), BF16), jax.ShapeDtypeStruct((t, CONV_DIM), F32),
                   jax.ShapeDtypeStruct((t, 2 * LANES), F32)],
        compiler_params=_cparams(("arbitrary",)),
        name="ssm_project",
    )(x, modt, norm_g3, w_z, w_x, w_dt)

    lane_pad = lambda v: jnp.pad(v, ((0, 0), (0, LANES - v.shape[-1]))).reshape(1, 2 * LANES)
    cw = jnp.pad(conv_w, ((0, SUBLANES - CONV_W), (0, 0)))
    cb = conv_b.reshape(1, CONV_DIM)
    dsk = jnp.repeat(d_skip, SSM_HEADDIM, axis=-1)
    prompt, sample = groups
    scan = functools.partial(_ssd_scan, xbc, dt)
    yp, new_state = scan(prompt, cw, cb, lane_pad(dt_bias), lane_pad(a_log), dsk, None, state_out)
    ys, = scan(sample, cw, cb, lane_pad(dt_bias), lane_pad(a_log), dsk, _pair_state(h0_sample), None)

    to = TOKEN_TILE
    row = lambda c: pl.BlockSpec((to, c), lambda i: (i, 0))
    y_specs, n_p = _two_group_specs(groups, to, D_INNER)
    x = pl.pallas_call(
        functools.partial(_ssm_out_kernel, n_prompt_tiles=n_p),
        grid=(t // to,),
        in_specs=y_specs + [row(D_INNER), _const_spec((1, D_INNER)),
                  _const_spec((D_INNER, D_MODEL), (j,)), row(D_MODEL), _mod_spec(groups, to, slot)],
        out_specs=row(D_MODEL),
        out_shape=jax.ShapeDtypeStruct((t, D_MODEL), F32),
        compiler_params=_cparams(("arbitrary",)),
        name="ssm_out",
    )(yp, ys, z, norm_g.reshape(1, D_INNER), w_out, x, modt)
    return x, new_state


def kernel(x_prompt, x_sample, cache_mla, state_ssm, c, c_ctx, mod_w, mod_b, norm_g, ffn_w_in, ffn_w_out,
           mla_w_in, mla_q_norm, mla_kv_norm, mla_wq_b, mla_wkv_b, mla_wo, ssm_w_in, ssm_conv_w, ssm_conv_b,
           ssm_dt_bias, ssm_a_log, ssm_d, ssm_norm_g, ssm_w_out, final_norm_g):
    nbp, lp, d = x_prompt.shape
    nbs, ls, _ = x_sample.shape
    prompt = _Group(0, nbp, lp)
    sample = _Group(nbp * lp, nbs, ls)
    groups = (prompt, sample)
    x = (x_prompt.reshape(-1, d), x_sample.reshape(-1, d))

    ncond = 1 + nbs
    sc = jnp.concatenate([c_ctx[None, :], c, jnp.zeros((SUBLANES - ncond, d), F32)], axis=0)
    mod = _modulation(sc, mod_w, mod_b)
    modt = mod[:, :ncond].reshape(DEPTH, ncond, 3, 3, d).transpose(1, 0, 2, 3, 4).reshape(ncond, DEPTH * 3, 3, d)

    norm_g3 = norm_g.reshape(DEPTH * 3, 1, d)
    ffn_in = ffn_w_in
    ffn_out = ffn_w_out
    n_ssm = ssm_w_in.shape[0]
    ssm_wz = ssm_w_in[:, :, :D_INNER].astype(BF16)
    ssm_wx = ssm_w_in[:, :, D_INNER:D_INNER + CONV_DIM].astype(BF16)
    wdt = ssm_w_in[:, :, D_INNER + CONV_DIM:].reshape(n_ssm, d, 2, SSM_HEADS)
    ssm_wdt = jnp.pad(wdt, ((0, 0), (0, 0), (0, 0), (0, LANES - SSM_HEADS))).reshape(n_ssm, d, 2 * LANES).astype(BF16)
    ssm_wo = ssm_w_out.astype(BF16)

    cos, sin = _rope_tables(groups)
    new_mla = []
    new_state = None
    tiles_p = prompt.rows // FFN_TILE
    tiles_s = sample.rows // FFN_TILE
    for i in range(DEPTH):
        j = i // 2
        x = _ffn_half(x, modt, groups, i, 0, norm_g3, ffn_in, ffn_out, final_norm_g)
        if i % 2 == 0:
            x, entry = _mla_layer(x, modt, groups, i, norm_g3, cache_mla[:, j], cos, sin,
                                  mla_w_in[j], mla_q_norm[j], mla_kv_norm[j], mla_wq_b[j], mla_wkv_b[j], mla_wo[j])
            new_mla.append(entry)
        else:
            x, new_state = _ssm_layer(x, modt, groups, i, norm_g3, state_ssm[:, j], (new_state, j),
                                      (j, ssm_wz, ssm_wx, ssm_wdt, ssm_wo), ssm_conv_w[j], ssm_conv_b[j],
                                      ssm_dt_bias[j], ssm_a_log[j], ssm_d[j], ssm_norm_g[j])
        if i < DEPTH - 1:
            x = _ffn_half(x, modt, groups, i, 1, norm_g3, ffn_in, ffn_out, final_norm_g)
    last = functools.partial(_ffn_half, x, modt, groups, DEPTH - 1, 1, norm_g3, ffn_in, ffn_out, final_norm_g, True)
    y_prompt = last((0, tiles_p)).reshape(nbp, lp, d)
    y_sample = last((tiles_p, tiles_s)).reshape(nbs, ls, d)
    new_state = new_state.reshape(nbp, n_ssm, 2, SSM_HEADS, SSM_HEADDIM, D_STATE)
    return y_prompt, y_sample, jnp.stack(new_mla, axis=1), new_state
```

```python
import functools
import math

import jax
import jax.numpy as jnp
from jax import lax
from jax.experimental import pallas as pl
from jax.experimental.pallas import tpu as pltpu

F32 = jnp.float32
BF16 = jnp.bfloat16

D_MODEL = 1024
DEPTH = 4
N_MOD = 9
D_FF = 2816
GRID_W = 64
MLA_HEADS = 16
Q_LORA = 512
KV_LORA = 256
QK_NOPE = 64
QK_ROPE = 32
V_HEAD = 64
QK_HEAD = QK_NOPE + QK_ROPE
ROPE_BASE = 10000.0
HEAD_PAD = 128
VT_ROWS = 80
D_INNER = 2048
SSM_HEADDIM = 64
SSM_HEADS = 32
SSM_GROUPS = 4
D_STATE = 128
CONV_W = 5
CONV_DIM = D_INNER + 2 * SSM_GROUPS * D_STATE
CHUNK = 128
EPS = 1e-6
LOG2E = math.log2(math.e)

LANES = 128
SUBLANES = 8
TOKEN_TILE = 512
FFN_TILE = 512
FF_CHUNK = 256
FFN_STAGE_SLOTS = 4
SSM_STAGE_SLOTS = 3
SSM_STAGE_ROWS = 128
VMEM_LIMIT = 56 * 1024 * 1024

NT = (((1,), (1,)), ((), ()))
TN = (((0,), (0,)), ((), ()))


def _cparams(sem):
    return pltpu.CompilerParams(dimension_semantics=sem, vmem_limit_bytes=VMEM_LIMIT)


def _const_spec(shape, index=None):
    index = tuple(index or ())
    block = (None,) * len(index) + tuple(shape)
    zeros = (0,) * len(shape)
    return pl.BlockSpec(block, lambda *_: index + zeros, pipeline_mode=pl.Buffered(1))


def _rmsnorm(x, g):
    ms = jnp.mean(x * x, axis=-1, keepdims=True)
    return (x * lax.rsqrt(ms + EPS)) * g


def _silu(x):
    h = 0.5 * x
    return h * jnp.tanh(h) + h


def _adaln(x, g, shift, scale):
    return _rmsnorm(x, g) * (1.0 + scale) + shift


class _Group:
    def __init__(self, row0, nb, length):
        self.row0, self.nb, self.length = row0, nb, length

    @property
    def rows(self):
        return self.nb * self.length


def _mod_kernel(sc_ref, w_ref, b_ref, o_ref):
    s = _silu(sc_ref[...])
    o_ref[...] = jnp.dot(s.astype(BF16), w_ref[...].astype(BF16),
                         preferred_element_type=F32) + b_ref[...]


def _modulation(sc, mod_w, mod_b):
    tn = 2304
    n = N_MOD * D_MODEL
    return pl.pallas_call(
        _mod_kernel,
        grid=(DEPTH, n // tn),
        in_specs=[
            pl.BlockSpec((SUBLANES, D_MODEL), lambda i, j: (0, 0)),
            pl.BlockSpec((None, D_MODEL, tn), lambda i, j: (i, 0, j)),
            pl.BlockSpec((None, 1, tn), lambda i, j: (i, 0, j)),
        ],
        out_specs=pl.BlockSpec((None, SUBLANES, tn), lambda i, j: (i, 0, j)),
        out_shape=jax.ShapeDtypeStruct((DEPTH, SUBLANES, n), F32),
        compiler_params=_cparams(("arbitrary", "arbitrary")),
        name="modulation",
    )(sc, mod_w, mod_b.reshape(DEPTH, 1, n))


def _mod_spec(groups, tile, slot, first_tile=0):
    bounds = []
    for g in groups:
        for b in range(g.nb if g.row0 else 1):
            bounds.append((g.row0 + b * g.length) // tile)

    def index_map(i, *_):
        gidx = 0
        for lo in bounds[1:]:
            gidx = gidx + (i + first_tile >= lo).astype(jnp.int32)
        return (gidx, slot, 0, 0)

    return pl.BlockSpec((None, None, 3, D_MODEL), index_map)


def _two_group_specs(groups, tile, width, axis=0):
    prompt, sample = groups
    n_p = prompt.rows // tile
    n_s = sample.rows // tile
    if axis == 0:
        return [pl.BlockSpec((tile, width), lambda i: (jnp.minimum(i, n_p - 1), 0)),
                pl.BlockSpec((tile, width), lambda i: (jnp.clip(i - n_p, 0, n_s - 1), 0))], n_p
    return [pl.BlockSpec((width, tile), lambda i: (0, jnp.minimum(i, n_p - 1))),
            pl.BlockSpec((width, tile), lambda i: (0, jnp.clip(i - n_p, 0, n_s - 1)))], n_p


def _ffn_kernel(*refs, layer, half, final_norm, n_prompt_tiles):
    if n_prompt_tiles is None:
        x_ref, refs = refs[0], refs[1:]
        load_x = lambda r0, r1: x_ref[r0:r1, :]
    else:
        xp_ref, xs_ref, refs = refs[0], refs[1], refs[2:]
        is_prompt = pl.program_id(0) < n_prompt_tiles
        load_x = lambda r0, r1: jnp.where(is_prompt, xp_ref[r0:r1, :], xs_ref[r0:r1, :])
    mod_ref, g_ref, fg_ref, win_hbm, wout_hbm, o_ref, win_b, wout_b, a_scr, stage_in, stage_out, sem = refs
    n_chunks = D_FF // FF_CHUNK
    n_slots = stage_out.shape[0]
    ahead = n_slots - 1

    def chunk_copies(c):
        lo = c * FF_CHUNK
        slot = c % n_slots
        w_in = win_hbm.at[layer, half]
        return (pltpu.make_async_copy(w_in.at[:, pl.ds(lo, FF_CHUNK)], stage_in.at[slot, 0], sem.at[slot, 0]),
                pltpu.make_async_copy(w_in.at[:, pl.ds(D_FF + lo, FF_CHUNK)], stage_in.at[slot, 1], sem.at[slot, 1]),
                pltpu.make_async_copy(wout_hbm.at[layer, half, pl.ds(lo, FF_CHUNK), :], stage_out.at[slot],
                                      sem.at[slot, 2]))

    m = mod_ref[...]

    def rows_pass(r0, r1, before_chunk=None):
        x = load_x(r0, r1)
        hb = _adaln(x, g_ref[...], m[0:1], m[1:2]).astype(BF16)
        for c in range(n_chunks):
            if before_chunk is not None:
                before_chunk(c)
            lo = c * FF_CHUNK
            gate = jnp.dot(hb, win_b[:, lo:lo + FF_CHUNK], preferred_element_type=F32)
            up = jnp.dot(hb, win_b[:, D_FF + lo:D_FF + lo + FF_CHUNK], preferred_element_type=F32)
            a_scr[r0:r1, lo:lo + FF_CHUNK] = (_silu(gate) * up).astype(BF16)
        y = jnp.dot(a_scr[r0:r1, :], wout_b[...], preferred_element_type=F32)
        out = x + (0.5 * m[2:3]) * y
        if final_norm:
            out = _rmsnorm(out, fg_ref[...])
        o_ref[r0:r1, :] = out

    def land_chunk(c):
        if c + ahead < n_chunks:
            for cp in chunk_copies(c + ahead):
                cp.start()
        for cp in chunk_copies(c):
            cp.wait()
        lo = c * FF_CHUNK
        slot = c % n_slots
        win_b[:, lo:lo + FF_CHUNK] = stage_in[slot, 0].astype(BF16)
        win_b[:, D_FF + lo:D_FF + lo + FF_CHUNK] = stage_in[slot, 1].astype(BF16)
        wout_b[lo:lo + FF_CHUNK, :] = stage_out[slot].astype(BF16)

    tm = o_ref.shape[0]

    @pl.when(pl.program_id(0) == 0)
    def _first_tile():
        for c in range(ahead):
            for cp in chunk_copies(c):
                cp.start()
        rows_pass(0, tm, land_chunk)

    @pl.when(pl.program_id(0) > 0)
    def _other_tiles():
        rows_pass(0, tm // 2)
        rows_pass(tm // 2, tm)


def _ffn_half(x, modt, groups, layer, half, norm_g3, w_in, w_out, final_g, final_norm=False, tiles=None):
    tm = FFN_TILE
    slot = 3 * layer + 2 * half
    if isinstance(x, tuple):
        x_specs, n_prompt_tiles = _two_group_specs(groups, tm, D_MODEL)
        first, count = 0, sum(g.rows for g in groups) // tm
    else:
        first, count = tiles or (0, x.shape[0] // tm)
        x_specs, n_prompt_tiles = [pl.BlockSpec((tm, D_MODEL), lambda i: (i + first, 0))], None
        x = (x,)
    return pl.pallas_call(
        functools.partial(_ffn_kernel, layer=layer, half=half, final_norm=final_norm,
                          n_prompt_tiles=n_prompt_tiles),
        grid=(count,),
        in_specs=x_specs + [
            _mod_spec(groups, tm, slot, first),
            _const_spec((1, D_MODEL), (slot,)),
            _const_spec((1, D_MODEL)),
            pl.BlockSpec(memory_space=pl.ANY),
            pl.BlockSpec(memory_space=pl.ANY),
        ],
        out_specs=pl.BlockSpec((tm, D_MODEL), lambda i: (i, 0)),
        out_shape=jax.ShapeDtypeStruct((count * tm, D_MODEL), F32),
        scratch_shapes=[pltpu.VMEM((D_MODEL, 2 * D_FF), BF16),
                        pltpu.VMEM((D_FF, D_MODEL), BF16),
                        pltpu.VMEM((tm, D_FF), BF16),
                        pltpu.VMEM((FFN_STAGE_SLOTS, 2, D_MODEL, FF_CHUNK), F32),
                        pltpu.VMEM((FFN_STAGE_SLOTS, FF_CHUNK, D_MODEL), F32),
                        pltpu.SemaphoreType.DMA((FFN_STAGE_SLOTS, 3))],
        compiler_params=_cparams(("arbitrary",)),
        name="ffn_half",
    )(*x, modt, norm_g3, final_g.reshape(1, D_MODEL), w_in, w_out)


def _expand_kv(ckv_b, krope, wk_ref, wvt_ref, k_ref, vt_ref):
    kn = jnp.dot(ckv_b, wk_ref[...], preferred_element_type=F32)
    for h in range(MLA_HEADS):
        lo = h * HEAD_PAD
        k_ref[:, lo:lo + HEAD_PAD] = (kn[:, lo:lo + HEAD_PAD] + krope).astype(BF16)
    vt_ref[...] = lax.dot_general(wvt_ref[...], ckv_b, NT, preferred_element_type=F32).astype(BF16)
    ones = jnp.ones((VT_ROWS - V_HEAD, ckv_b.shape[0]), BF16)
    for h in range(MLA_HEADS):
        vt_ref[h * VT_ROWS + V_HEAD:(h + 1) * VT_ROWS, :] = ones


def _mla_proj_kernel(x_ref, mod_ref, g_ref, win_ref, qn_ref, kvn_ref, wq1_ref,
                     wk_ref, wvt_ref, cos_ref, sin_ref,
                     q_ref, k_ref, vt_ref, ckv_ref, kr_ref):
    x = x_ref[...]
    m = mod_ref[...]
    hb = _adaln(x, g_ref[...], m[0:1], m[1:2]).astype(BF16)
    a = jnp.dot(hb, win_ref[...], preferred_element_type=F32)
    qa = _rmsnorm(a[:, :Q_LORA], qn_ref[...]).astype(BF16)
    ckv = _rmsnorm(a[:, Q_LORA:Q_LORA + KV_LORA], kvn_ref[...])
    kr = a[:, Q_LORA + KV_LORA:Q_LORA + KV_LORA + HEAD_PAD]
    kr_rot = a[:, Q_LORA + KV_LORA + HEAD_PAD:]
    cosk = cos_ref[...]
    sink = sin_ref[...]
    lane = lax.broadcasted_iota(jnp.int32, cosk.shape, 1)
    scale = LOG2E / math.sqrt(QK_HEAD)
    cosq = (cosk + (lane < QK_NOPE).astype(F32)) * scale
    q_rot = QK_ROPE // 4
    first = ((lane - QK_NOPE) % (2 * q_rot)) < q_rot
    sin_dn = jnp.where(first, -sink, 0.0) * scale
    sin_up = jnp.where(first, 0.0, sink) * scale
    q1 = jnp.dot(qa, wq1_ref[...], preferred_element_type=F32)
    for h in range(MLA_HEADS):
        lo = h * HEAD_PAD
        qh = q1[:, lo:lo + HEAD_PAD]
        q_ref[:, lo:lo + HEAD_PAD] = (qh * cosq + pltpu.roll(qh, HEAD_PAD - q_rot, 1) * sin_dn
                                      + pltpu.roll(qh, q_rot, 1) * sin_up).astype(BF16)
    ckv_ref[...] = ckv
    kr_ref[...] = kr
    _expand_kv(ckv.astype(BF16), kr * cosk + kr_rot * sink, wk_ref, wvt_ref, k_ref, vt_ref)


def _mla_ctx_kernel(ckv_ref, kr_ref, wk_ref, wvt_ref, k_ref, vt_ref):
    _expand_kv(ckv_ref[...].astype(BF16), kr_ref[...], wk_ref, wvt_ref, k_ref, vt_ref)


def _mla_attn_kernel(*refs, n_parts, heads):
    q_ref = refs[0]
    kv_refs = refs[1:1 + 2 * n_parts]
    o_ref = refs[1 + 2 * n_parts]
    def scores(h):
        qh = q_ref[:, h * HEAD_PAD:(h + 1) * HEAD_PAD]
        return [lax.dot_general(kv_refs[2 * i][:, h * HEAD_PAD:(h + 1) * HEAD_PAD], qh, NT,
                                preferred_element_type=F32) for i in range(n_parts)]

    ss_next = scores(0)
    for h in range(heads):
        ss = ss_next
        if h + 1 < heads:
            ss_next = scores(h + 1)
        mx = ss[0].max(axis=0, keepdims=True)
        for s in ss[1:]:
            mx = jnp.maximum(mx, s.max(axis=0, keepdims=True))
        acc = None
        for i, s in enumerate(ss):
            p = jnp.exp2(s - mx).astype(BF16)
            o = jnp.dot(kv_refs[2 * i + 1][h * VT_ROWS:(h + 1) * VT_ROWS, :], p, preferred_element_type=F32)
            acc = o if acc is None else acc + o
        o_ref[h * V_HEAD:(h + 1) * V_HEAD, :] = (acc[:V_HEAD] / acc[V_HEAD:V_HEAD + 1]).astype(BF16)


def _mla_attention(q, parts, grp, tq, heads):
    nq = grp.length // tq
    qoff = grp.row0 // tq
    in_specs = [pl.BlockSpec((tq, heads * HEAD_PAD), lambda b, p, i: (qoff + b * nq + i, p))]
    args = [q]
    for k, vt, row0, lk in parts:
        koff = row0 // lk
        in_specs.append(pl.BlockSpec((lk, heads * HEAD_PAD), lambda b, p, i, koff=koff: (koff + b, p)))
        in_specs.append(pl.BlockSpec((heads * VT_ROWS, lk), lambda b, p, i, koff=koff: (p, koff + b)))
        args += [k, vt]
    return pl.pallas_call(
        functools.partial(_mla_attn_kernel, n_parts=len(parts), heads=heads),
        grid=(grp.nb, MLA_HEADS // heads, nq),
        in_specs=in_specs,
        out_specs=pl.BlockSpec((heads * V_HEAD, tq), lambda b, p, i: (p, b * nq + i)),
        out_shape=jax.ShapeDtypeStruct((MLA_HEADS * V_HEAD, grp.rows), BF16),
        compiler_params=_cparams(("arbitrary", "arbitrary", "arbitrary")),
        name="mla_attention",
    )(*args)


def _mla_out_kernel(otp_ref, ots_ref, wo_ref, x_ref, mod_ref, o_ref, *, n_prompt_tiles):
    o_t = jnp.where(pl.program_id(0) < n_prompt_tiles, otp_ref[...], ots_ref[...])
    y = lax.dot_general(o_t, wo_ref[...], TN, preferred_element_type=F32)
    o_ref[...] = x_ref[...] + mod_ref[...][2:3] * y


def _rope_tables(groups):
    prompt, sample = groups
    half = QK_ROPE // 2
    freqs = 1.0 / (ROPE_BASE ** (jnp.arange(0, half, 2, dtype=F32) / half))
    pos_i = jnp.arange(sample.length)
    pos = jnp.stack([pos_i // GRID_W, pos_i % GRID_W], axis=-1).astype(F32)
    ang = pos[:, :, None] * freqs
    ang = jnp.broadcast_to(ang[:, :, None, :], (sample.length, 2, 2, half // 2)).reshape(sample.length, QK_ROPE)
    lanes = lambda v: jnp.pad(v, ((0, 0), (QK_NOPE, HEAD_PAD - QK_HEAD)))
    cos = jnp.concatenate([lanes(jnp.ones((prompt.rows, QK_ROPE), F32))] + [lanes(jnp.cos(ang))] * sample.nb, axis=0)
    sin = jnp.concatenate([jnp.zeros((prompt.rows, HEAD_PAD), F32)] + [lanes(jnp.sin(ang))] * sample.nb, axis=0)
    return cos, sin


def _rot_cols(w):
    q = QK_ROPE // 4
    w4 = w.reshape(w.shape[:-1] + (2, 2, q))
    return jnp.stack([-w4[..., 1, :], w4[..., 0, :]], axis=-2).reshape(w.shape)


def _mla_weights(w_in, wq_b, wkv_b):
    d = w_in.shape[0]
    z = lambda *s: jnp.zeros(s, F32)
    k_r = w_in[:, Q_LORA + KV_LORA:]
    pad_tile = lambda c: jnp.concatenate([z(d, QK_NOPE), c, z(d, HEAD_PAD - QK_HEAD)], axis=1)
    w_in_ext = jnp.concatenate([w_in[:, :Q_LORA + KV_LORA], pad_tile(k_r), pad_tile(_rot_cols(k_r))], axis=1)
    wq = wq_b.reshape(Q_LORA, MLA_HEADS, QK_HEAD)
    zq = lambda n: z(Q_LORA, MLA_HEADS, n)
    wq1 = jnp.concatenate([wq, zq(HEAD_PAD - QK_HEAD)], axis=-1)
    wkv = wkv_b.reshape(KV_LORA, MLA_HEADS, QK_NOPE + V_HEAD)
    wk = jnp.concatenate([wkv[..., :QK_NOPE], z(KV_LORA, MLA_HEADS, HEAD_PAD - QK_NOPE)], axis=-1)
    wv = jnp.concatenate([wkv[..., QK_NOPE:], z(KV_LORA, MLA_HEADS, VT_ROWS - V_HEAD)], axis=-1)
    wvt = wv.reshape(KV_LORA, MLA_HEADS * VT_ROWS).T
    flat = lambda w: w.reshape(w.shape[0], -1).astype(BF16)
    return w_in_ext.astype(BF16), flat(wq1), flat(wk), wvt.astype(BF16)


def _mla_layer(x, modt, groups, layer, norm_g3, cache, cos, sin, w_in, q_norm, kv_norm, wq_b, wkv_b, wo):
    t = x.shape[0]
    tm = TOKEN_TILE
    slot = 3 * layer + 1
    w_in_ext, wq1, wk, wvt = _mla_weights(w_in, wq_b, wkv_b)
    hd = MLA_HEADS * HEAD_PAD
    vr = MLA_HEADS * VT_ROWS
    row = lambda c: pl.BlockSpec((tm, c), lambda i: (i, 0))
    q, k, vt, ckv, kr = pl.pallas_call(
        _mla_proj_kernel,
        grid=(t // tm,),
        in_specs=[
            row(D_MODEL), _mod_spec(groups, tm, slot), _const_spec((1, D_MODEL), (slot,)),
            _const_spec(w_in_ext.shape), _const_spec((1, Q_LORA)), _const_spec((1, KV_LORA)),
            _const_spec(wq1.shape), _const_spec(wk.shape), _const_spec(wvt.shape),
            row(HEAD_PAD), row(HEAD_PAD),
        ],
        out_specs=[row(hd), row(hd), pl.BlockSpec((vr, tm), lambda i: (0, i)), row(KV_LORA), row(HEAD_PAD)],
        out_shape=[jax.ShapeDtypeStruct((t, hd), BF16), jax.ShapeDtypeStruct((t, hd), BF16),
                   jax.ShapeDtypeStruct((vr, t), BF16), jax.ShapeDtypeStruct((t, KV_LORA), F32),
                   jax.ShapeDtypeStruct((t, HEAD_PAD), F32)],
        compiler_params=_cparams(("arbitrary",)),
        name="mla_project",
    )(x, modt, norm_g3, w_in_ext, q_norm.reshape(1, -1), kv_norm.reshape(1, -1),
      wq1, wk, wvt, cos, sin)

    prompt, sample = groups
    nb, past, _ = cache.shape
    cflat = cache.reshape(nb * past, -1)
    ckv_c = cflat[:, :KV_LORA]
    kr_c = jnp.pad(cflat[:, KV_LORA:], ((0, 0), (QK_NOPE, HEAD_PAD - QK_HEAD)))
    crow = lambda c: pl.BlockSpec((past, c), lambda i: (i, 0))
    k_c, vt_c = pl.pallas_call(
        _mla_ctx_kernel,
        grid=(nb,),
        in_specs=[crow(KV_LORA), crow(HEAD_PAD), _const_spec(wk.shape), _const_spec(wvt.shape)],
        out_specs=[crow(hd), pl.BlockSpec((vr, past), lambda i: (0, i))],
        out_shape=[jax.ShapeDtypeStruct((nb * past, hd), BF16), jax.ShapeDtypeStruct((vr, nb * past), BF16)],
        compiler_params=_cparams(("arbitrary",)),
        name="mla_ctx_expand",
    )(ckv_c, kr_c, wk, wvt)

    ot_p = _mla_attention(q, [(k, vt, prompt.row0, prompt.length)], prompt, 256, MLA_HEADS)
    ot_s = _mla_attention(q, [(k_c, vt_c, 0, past), (k, vt, sample.row0, sample.length)], sample, 512, 8)
    vd = MLA_HEADS * V_HEAD
    o_specs, n_p = _two_group_specs(groups, tm, vd, axis=1)
    x = pl.pallas_call(
        functools.partial(_mla_out_kernel, n_prompt_tiles=n_p),
        grid=(t // tm,),
        in_specs=o_specs + [_const_spec((vd, D_MODEL)), row(D_MODEL), _mod_spec(groups, tm, slot)],
        out_specs=row(D_MODEL),
        out_shape=jax.ShapeDtypeStruct((t, D_MODEL), F32),
        compiler_params=_cparams(("arbitrary",)),
        name="mla_out",
    )(ot_p, ot_s, wo.astype(BF16), x, modt)
    entry = jnp.concatenate([ckv[:prompt.rows], kr[:prompt.rows, QK_NOPE:QK_HEAD]], axis=-1)
    return x, entry.reshape(prompt.nb, prompt.length, -1)


def _ssm_proj_kernel(x_ref, mod_ref, g_ref, wdt_ref, win_hbm, z_ref, xbc_ref, dt_ref, wzx_b, stage, sem, *, j):
    rows = stage.shape[1]
    n_slots = stage.shape[0]
    ahead = n_slots - 1
    n_chunks = D_MODEL // rows

    def slab_copy(c):
        return pltpu.make_async_copy(win_hbm.at[j, pl.ds(c * rows, rows), :], stage.at[c % n_slots],
                                     sem.at[c % n_slots])

    @pl.when(pl.program_id(0) == 0)
    def _load_weights():
        for c in range(ahead):
            slab_copy(c).start()
        for c in range(n_chunks):
            if c + ahead < n_chunks:
                slab_copy(c + ahead).start()
            slab_copy(c).wait()
            wzx_b[c * rows:(c + 1) * rows, :] = stage[c % n_slots][:, :D_INNER + CONV_DIM].astype(BF16)

    m = mod_ref[...]
    hb = _adaln(x_ref[...], g_ref[...], m[0:1], m[1:2]).astype(BF16)
    z_ref[...] = jnp.dot(hb, wzx_b[:, :D_INNER], preferred_element_type=F32).astype(BF16)
    xbc_ref[...] = jnp.dot(hb, wzx_b[:, D_INNER:], preferred_element_type=F32)
    dt_ref[...] = jnp.dot(hb, wdt_ref[...], preferred_element_type=F32)


def _split3(a):
    hi = a.astype(BF16)
    r1 = a - hi.astype(F32)
    mid = r1.astype(BF16)
    lo = (r1 - mid.astype(F32)).astype(BF16)
    return hi, mid, lo


def _ssd_kernel(*refs, zero_init, emit_state, layer, nc):
    it = iter(refs)
    cur_ref, prev_ref, next_ref, dt_ref = next(it), next(it), next(it), next(it)
    cw_ref, cb_ref, dtb_ref, alog_ref, dskip_ref = next(it), next(it), next(it), next(it), next(it)
    h0_ref = None if zero_init else next(it)
    prev_st_ref = next(it) if layer else None
    y_ref = next(it)
    st_ref = next(it) if emit_state else None
    s_f, s_b, sb_loc, y_keep, c_keep, ecb_keep, etb_keep, xpad, act = (next(it) for _ in range(9))

    q = CHUNK
    step = pl.program_id(1)
    npairs = SSM_HEADS // 2
    hpg = SSM_HEADS // SSM_GROUPS
    lane = lax.broadcasted_iota(jnp.int32, (q, LANES), 1)
    left = lane < SSM_HEADDIM

    def pair_cols(v, p):
        return jnp.where(left[:v.shape[0]], v[:, 2 * p:2 * p + 1], v[:, 2 * p + 1:2 * p + 2])

    @pl.when(step == 0)
    def _init():
        if zero_init:
            s_f[...] = jnp.zeros(s_f.shape, F32)
            s_b[...] = jnp.zeros(s_b.shape, F32)
        else:
            s_f[...] = h0_ref[0]
            s_b[...] = h0_ref[1]

    @pl.when(step < nc)
    def _sweep_up():
        c = step
        blk = 4 * SUBLANES
        st = blk // SUBLANES
        has_prev = c > 0
        has_next = c < nc - 1
        for j in range(CONV_DIM // LANES):
            cols = slice(j * LANES, (j + 1) * LANES)
            xpad[j, 0:SUBLANES, :] = jnp.where(has_prev, prev_ref[:, cols], 0.0)
            xpad[j, SUBLANES:SUBLANES + q, :] = cur_ref[:, cols]
            xpad[j, SUBLANES + q:, :] = jnp.where(has_next, next_ref[:, cols], 0.0)
            wk = [jnp.broadcast_to(cw_ref[k:k + 1, cols], (SUBLANES, LANES)) for k in range(CONV_W)]
            bias = jnp.broadcast_to(cb_ref[:, cols], (SUBLANES, LANES))
            for b in range(q // blk):
                r0 = SUBLANES + b * blk - CONV_W // 2
                taps = [xpad[j, pl.ds(r0 + u, SUBLANES, stride=st), :] for u in range(st + CONV_W - 1)]
                for v in range(st):
                    acc = bias
                    for k in range(CONV_W):
                        acc = acc + taps[v + k] * wk[k]
                    act[j, pl.ds(b * blk + v, SUBLANES, stride=st), :] = _silu(acc)
        n_x = D_INNER // LANES
        for g in range(SSM_GROUPS):
            c_keep[c, :, g * D_STATE:(g + 1) * D_STATE] = act[n_x + SSM_GROUPS + g].astype(BF16)

        dtr = dt_ref[...] + dtb_ref[...]
        dtv = jnp.maximum(dtr, 0.0) + jnp.log1p(jnp.exp(-jnp.abs(dtr)))
        a2 = dtv * (-LOG2E * jnp.exp(alog_ref[...]))
        ri = lax.broadcasted_iota(jnp.int32, (q, q), 0)
        ci = lax.broadcasted_iota(jnp.int32, (q, q), 1)
        lower = ci <= ri
        slower = ci < ri
        supper = ci > ri
        tot = jnp.sum(a2, axis=0, keepdims=True)
        dirs = []
        for d, keep in enumerate((lower, ci >= ri)):
            ad = a2[:, d * LANES:(d + 1) * LANES]
            dtd = dtv[:, d * LANES:(d + 1) * LANES]
            td = tot[:, d * LANES:(d + 1) * LANES]
            tri = keep.astype(BF16)
            cum = sum(jnp.dot(tri, part, preferred_element_type=F32) for part in _split3(ad))
            dirs.append(dict(cum=cum, cum_t=cum.T, dt_t=dtd.T, ecum=jnp.exp2(cum), etot=jnp.exp2(td),
                             wout_t=(jnp.exp2(td - cum) * dtd).T))
        fw, bw = dirs
        dt_sum_t = fw["dt_t"] + bw["dt_t"]
        ecb_keep[c] = bw["ecum"]
        etb_keep[c] = jnp.broadcast_to(bw["etot"], (SUBLANES, LANES))
        dsum = dskip_ref[0:1, :] + dskip_ref[1:2, :]

        for g in range(SSM_GROUPS):
            bm = act[n_x + g]
            cm_b = c_keep[c, :, g * D_STATE:(g + 1) * D_STATE]
            bm_t = bm.T
            cb = lax.dot_general(cm_b, bm.astype(BF16), NT, preferred_element_type=F32)
            for pp in range(hpg // 2):
                p = g * (hpg // 2) + pp
                xp = act[p]
                xbd = jnp.concatenate([jnp.where(left, xp, 0.0), jnp.where(left, 0.0, xp)], axis=0).astype(BF16)
                ms, bfs, bbs = [], [], []
                for h in (2 * p, 2 * p + 1):
                    seg = jnp.where(lower, fw["cum"][:, h:h + 1] - fw["cum_t"][h:h + 1, :],
                                    bw["cum"][:, h:h + 1] - bw["cum_t"][h:h + 1, :])
                    wgt = jnp.where(slower, fw["dt_t"][h:h + 1, :],
                                    jnp.where(supper, bw["dt_t"][h:h + 1, :], dt_sum_t[h:h + 1, :]))
                    ms.append((cb * jnp.exp2(seg) * wgt).astype(BF16))
                    bfs.append((bm_t * fw["wout_t"][h:h + 1, :]).astype(BF16))
                    bbs.append((bm_t * bw["wout_t"][h:h + 1, :]).astype(BF16))
                y = jnp.dot(jnp.concatenate(ms, axis=1), xbd, preferred_element_type=F32)
                st = s_f[p]
                y_off = jnp.dot(cm_b, st.astype(BF16), preferred_element_type=F32)
                y_keep[c, :, p * LANES:(p + 1) * LANES] = (y + y_off * pair_cols(fw["ecum"], p)
                                                        + dsum[:, p * LANES:(p + 1) * LANES] * xp)
                s_f[p] = st * pair_cols(fw["etot"], p) + jnp.dot(jnp.concatenate(bfs, axis=1), xbd,
                                                                 preferred_element_type=F32)
                sb_loc[c, p] = jnp.dot(jnp.concatenate(bbs, axis=1), xbd, preferred_element_type=F32)

    @pl.when(step >= nc)
    def _sweep_down():
        c = 2 * nc - 1 - step
        ecb = ecb_keep[c]
        etb = etb_keep[c][0:1]
        for p in range(npairs):
            g = p // (hpg // 2)
            st = s_b[p]
            y_off = jnp.dot(c_keep[c, :, g * D_STATE:(g + 1) * D_STATE], st.astype(BF16),
                            preferred_element_type=F32)
            y_ref[:, p * LANES:(p + 1) * LANES] = (y_keep[c, :, p * LANES:(p + 1) * LANES]
                                                    + y_off * pair_cols(ecb, p))
            s_b[p] = st * pair_cols(etb, p) + sb_loc[c, p]

        if emit_state:
            @pl.when(step == 2 * nc - 1)
            def _emit():
                for l in range(layer):
                    st_ref[l] = prev_st_ref[l]
                for p in range(npairs):
                    st_ref[layer, 0, p] = s_f[p].T
                    st_ref[layer, 1, p] = s_b[p].T


def _ssd_scan(xbc, dt, grp, conv_w, conv_b, dt_bias, a_log, d_skip, h0, state_out):
    t = xbc.shape[0]
    q = CHUNK
    nc = grp.length // q
    zero_init = h0 is None
    emit_state = state_out is not None
    boff = grp.row0 // q
    npairs = SSM_HEADS // 2
    qs = q // SUBLANES

    chunk = lambda b, s: boff + b * nc + jnp.minimum(s, nc - 1)
    in_specs = [
        pl.BlockSpec((q, CONV_DIM), lambda b, s: (chunk(b, s), 0)),
        pl.BlockSpec((SUBLANES, CONV_DIM), lambda b, s: (jnp.maximum(chunk(b, s) * qs - 1, 0), 0)),
        pl.BlockSpec((SUBLANES, CONV_DIM), lambda b, s: (jnp.minimum((chunk(b, s) + 1) * qs, t // SUBLANES - 1), 0)),
        pl.BlockSpec((q, 2 * LANES), lambda b, s: (chunk(b, s), 0)),
        _const_spec((SUBLANES, CONV_DIM)), _const_spec((1, CONV_DIM)),
        _const_spec((1, 2 * LANES)), _const_spec((1, 2 * LANES)), _const_spec((2, D_INNER)),
    ]
    args = [xbc, xbc, xbc, dt, conv_w, conv_b, dt_bias, a_log, d_skip]
    if not zero_init:
        in_specs.append(pl.BlockSpec((None, 2, npairs, D_STATE, LANES), lambda b, s: (b, 0, 0, 0, 0)))
        args.append(h0)
    state_block = lambda n: pl.BlockSpec((None, n, 2, npairs, LANES, D_STATE), lambda b, s: (b, 0, 0, 0, 0, 0))
    layer = state_out[1] if emit_state else 0
    if layer:
        in_specs.append(state_block(layer))
        args.append(state_out[0])
    out_specs = [
        pl.BlockSpec((q, D_INNER), lambda b, s: (b * nc + jnp.where(s < nc, nc - 1, 2 * nc - 1 - s), 0)),
    ]
    out_shape = [jax.ShapeDtypeStruct((grp.rows, D_INNER), F32)]
    if emit_state:
        out_specs.append(state_block(layer + 1))
        out_shape.append(jax.ShapeDtypeStruct((grp.nb, layer + 1, 2, npairs, LANES, D_STATE), F32))
    return pl.pallas_call(
        functools.partial(_ssd_kernel, zero_init=zero_init, emit_state=emit_state, layer=layer, nc=nc),
        grid=(grp.nb, 2 * nc),
        in_specs=in_specs,
        out_specs=out_specs,
        out_shape=out_shape,
        scratch_shapes=[pltpu.VMEM((npairs, D_STATE, LANES), F32),
                        pltpu.VMEM((npairs, D_STATE, LANES), F32),
                        pltpu.VMEM((nc, npairs, D_STATE, LANES), F32),
                        pltpu.VMEM((nc, q, D_INNER), F32),
                        pltpu.VMEM((nc, q, SSM_GROUPS * D_STATE), BF16),
                        pltpu.VMEM((nc, q, LANES), F32),
                        pltpu.VMEM((nc, SUBLANES, LANES), F32),
                        pltpu.VMEM((CONV_DIM // LANES, q + 2 * SUBLANES, LANES), F32),
                        pltpu.VMEM((CONV_DIM // LANES, q, LANES), F32)],
        compiler_params=_cparams(("arbitrary", "arbitrary")),
        name="ssd_scan",
    )(*args)


def _ssm_out_kernel(yp_ref, ys_ref, z_ref, ng_ref, w_ref, x_ref, mod_ref, o_ref, *, n_prompt_tiles):
    y = jnp.where(pl.program_id(0) < n_prompt_tiles, yp_ref[...], ys_ref[...])
    y = y * _silu(z_ref[...].astype(F32))
    yn = _rmsnorm(y, ng_ref[...]).astype(BF16)
    o_ref[...] = x_ref[...] + mod_ref[...][2:3] * jnp.dot(yn, w_ref[...], preferred_element_type=F32)


def _pair_state(h):
    b = h.shape[0]
    return h.reshape(b, 2, SSM_HEADS // 2, 2, SSM_HEADDIM, D_STATE).transpose(0, 1, 2, 5, 3, 4).reshape(
        b, 2, SSM_HEADS // 2, D_STATE, 2 * SSM_HEADDIM)


def _ssm_layer(x, modt, groups, layer, norm_g3, h0_sample, state_out, weights, conv_w, conv_b,
               dt_bias, a_log, d_skip, norm_g):
    j, w_in, w_dt, w_out = weights
    t = x.shape[0]
    tm = TOKEN_TILE
    slot = 3 * layer + 1
    row = lambda c: pl.BlockSpec((tm, c), lambda i: (i, 0))
    z, xbc, dt = pl.pallas_call(
        functools.partial(_ssm_proj_kernel, j=j),
        grid=(t // tm,),
        in_specs=[row(D_MODEL), _mod_spec(groups, tm, slot), _const_spec((1, D_MODEL), (slot,)),
                  _const_spec(w_dt.shape[1:], (j,)), pl.BlockSpec(memory_space=pl.ANY)],
        out_specs=[row(D_INNER), row(CONV_DIM), row(2 * LANES)],
        out_shape=[jax.ShapeDtypeStruct((t, D_INNER), BF16), jax.ShapeDtypeStruct((t, CONV_DIM), F32),
                   jax.ShapeDtypeStruct((t, 2 * LANES), F32)],
        scratch_shapes=[pltpu.VMEM((D_MODEL, D_INNER + CONV_DIM), BF16),
                        pltpu.VMEM((SSM_STAGE_SLOTS, SSM_STAGE_ROWS, w_in.shape[-1]), F32),
                        pltpu.SemaphoreType.DMA((SSM_STAGE_SLOTS,))],
        compiler_params=_cparams(("arbitrary",)),
        name="ssm_project",
    )(x, modt, norm_g3, w_dt, w_in)

    lane_pad = lambda v: jnp.pad(v, ((0, 0), (0, LANES - v.shape[-1]))).reshape(1, 2 * LANES)
    cw = jnp.pad(conv_w, ((0, SUBLANES - CONV_W), (0, 0)))
    cb = conv_b.reshape(1, CONV_DIM)
    dsk = jnp.repeat(d_skip, SSM_HEADDIM, axis=-1)
    prompt, sample = groups
    scan = functools.partial(_ssd_scan, xbc, dt)
    yp, new_state = scan(prompt, cw, cb, lane_pad(dt_bias), lane_pad(a_log), dsk, None, state_out)
    ys, = scan(sample, cw, cb, lane_pad(dt_bias), lane_pad(a_log), dsk, _pair_state(h0_sample), None)

    to = TOKEN_TILE
    row = lambda c: pl.BlockSpec((to, c), lambda i: (i, 0))
    y_specs, n_p = _two_group_specs(groups, to, D_INNER)
    x = pl.pallas_call(
        functools.partial(_ssm_out_kernel, n_prompt_tiles=n_p),
        grid=(t // to,),
        in_specs=y_specs + [row(D_INNER), _const_spec((1, D_INNER)),
                  _const_spec((D_INNER, D_MODEL), (j,)), row(D_MODEL), _mod_spec(groups, to, slot)],
        out_specs=row(D_MODEL),
        out_shape=jax.ShapeDtypeStruct((t, D_MODEL), F32),
        compiler_params=_cparams(("arbitrary",)),
        name="ssm_out",
    )(yp, ys, z, norm_g.reshape(1, D_INNER), w_out, x, modt)
    return x, new_state


def kernel(x_prompt, x_sample, cache_mla, state_ssm, c, c_ctx, mod_w, mod_b, norm_g, ffn_w_in, ffn_w_out,
           mla_w_in, mla_q_norm, mla_kv_norm, mla_wq_b, mla_wkv_b, mla_wo, ssm_w_in, ssm_conv_w, ssm_conv_b,
           ssm_dt_bias, ssm_a_log, ssm_d, ssm_norm_g, ssm_w_out, final_norm_g):
    nbp, lp, d = x_prompt.shape
    nbs, ls, _ = x_sample.shape
    prompt = _Group(0, nbp, lp)
    sample = _Group(nbp * lp, nbs, ls)
    groups = (prompt, sample)
    x = (x_prompt.reshape(-1, d), x_sample.reshape(-1, d))

    ncond = 1 + nbs
    sc = jnp.concatenate([c_ctx[None, :], c, jnp.zeros((SUBLANES - ncond, d), F32)], axis=0)
    mod = _modulation(sc, mod_w, mod_b)
    modt = mod[:, :ncond].reshape(DEPTH, ncond, 3, 3, d).transpose(1, 0, 2, 3, 4).reshape(ncond, DEPTH * 3, 3, d)

    norm_g3 = norm_g.reshape(DEPTH * 3, 1, d)
    ffn_in = ffn_w_in
    ffn_out = ffn_w_out
    n_ssm = ssm_w_in.shape[0]
    wdt = ssm_w_in[:, :, D_INNER + CONV_DIM:].reshape(n_ssm, d, 2, SSM_HEADS)
    ssm_wdt = jnp.pad(wdt, ((0, 0), (0, 0), (0, 0), (0, LANES - SSM_HEADS))).reshape(n_ssm, d, 2 * LANES).astype(BF16)
    ssm_wo = ssm_w_out.astype(BF16)

    cos, sin = _rope_tables(groups)
    new_mla = []
    new_state = None
    tiles_p = prompt.rows // FFN_TILE
    tiles_s = sample.rows // FFN_TILE
    for i in range(DEPTH):
        j = i // 2
        x = _ffn_half(x, modt, groups, i, 0, norm_g3, ffn_in, ffn_out, final_norm_g)
        if i % 2 == 0:
            x, entry = _mla_layer(x, modt, groups, i, norm_g3, cache_mla[:, j], cos, sin,
                                  mla_w_in[j], mla_q_norm[j], mla_kv_norm[j], mla_wq_b[j], mla_wkv_b[j], mla_wo[j])
            new_mla.append(entry)
        else:
            x, new_state = _ssm_layer(x, modt, groups, i, norm_g3, state_ssm[:, j], (new_state, j),
                                      (j, ssm_w_in, ssm_wdt, ssm_wo), ssm_conv_w[j], ssm_conv_b[j],
                                      ssm_dt_bias[j], ssm_a_log[j], ssm_d[j], ssm_norm_g[j])
        if i < DEPTH - 1:
            x = _ffn_half(x, modt, groups, i, 1, norm_g3, ffn_in, ffn_out, final_norm_g)
    last = functools.partial(_ffn_half, x, modt, groups, DEPTH - 1, 1, norm_g3, ffn_in, ffn_out, final_norm_g, True)
    y_prompt = last((0, tiles_p)).reshape(nbp, lp, d)
    y_sample = last((tiles_p, tiles_s)).reshape(nbs, ls, d)
    new_state = new_state.reshape(nbp, n_ssm, 2, SSM_HEADS, SSM_HEADDIM, D_STATE)
    return y_prompt, y_sample, jnp.stack(new_mla, axis=1), new_state
```

```python
import functools
import math

import jax
import jax.numpy as jnp
from jax import lax
from jax.experimental import pallas as pl
from jax.experimental.pallas import tpu as pltpu

F32 = jnp.float32
BF16 = jnp.bfloat16

D_MODEL = 1024
DEPTH = 4
N_MOD = 9
D_FF = 2816
GRID_W = 64
MLA_HEADS = 16
Q_LORA = 512
KV_LORA = 256
QK_NOPE = 64
QK_ROPE = 32
V_HEAD = 64
QK_HEAD = QK_NOPE + QK_ROPE
ROPE_BASE = 10000.0
HEAD_PAD = 128
VT_ROWS = 80
D_INNER = 2048
SSM_HEADDIM = 64
SSM_HEADS = 32
SSM_GROUPS = 4
D_STATE = 128
CONV_W = 5
CONV_DIM = D_INNER + 2 * SSM_GROUPS * D_STATE
CHUNK = 128
EPS = 1e-6
LOG2E = math.log2(math.e)

LANES = 128
SUBLANES = 8
TOKEN_TILE = 512
FFN_TILE = 512
FF_CHUNK = 256
FFN_STAGE_SLOTS = 4
SSM_STAGE_SLOTS = 4
SSM_STAGE_ROWS = 128
VMEM_LIMIT = 56 * 1024 * 1024

NT = (((1,), (1,)), ((), ()))
TN = (((0,), (0,)), ((), ()))


def _cparams(sem):
    return pltpu.CompilerParams(dimension_semantics=sem, vmem_limit_bytes=VMEM_LIMIT)


def _const_spec(shape, index=None):
    index = tuple(index or ())
    block = (None,) * len(index) + tuple(shape)
    zeros = (0,) * len(shape)
    return pl.BlockSpec(block, lambda *_: index + zeros, pipeline_mode=pl.Buffered(1))


def _rmsnorm(x, g):
    ms = jnp.mean(x * x, axis=-1, keepdims=True)
    return (x * lax.rsqrt(ms + EPS)) * g


def _silu(x):
    h = 0.5 * x
    return h * jnp.tanh(h) + h


def _adaln(x, g, shift, scale):
    return _rmsnorm(x, g) * (1.0 + scale) + shift


class _Group:
    def __init__(self, row0, nb, length):
        self.row0, self.nb, self.length = row0, nb, length

    @property
    def rows(self):
        return self.nb * self.length


def _mod_kernel(sc_ref, w_ref, b_ref, o_ref):
    s = _silu(sc_ref[...])
    o_ref[...] = jnp.dot(s.astype(BF16), w_ref[...].astype(BF16),
                         preferred_element_type=F32) + b_ref[...]


def _modulation(sc, mod_w, mod_b):
    tn = 2304
    n = N_MOD * D_MODEL
    return pl.pallas_call(
        _mod_kernel,
        grid=(DEPTH, n // tn),
        in_specs=[
            pl.BlockSpec((SUBLANES, D_MODEL), lambda i, j: (0, 0)),
            pl.BlockSpec((None, D_MODEL, tn), lambda i, j: (i, 0, j)),
            pl.BlockSpec((None, 1, tn), lambda i, j: (i, 0, j)),
        ],
        out_specs=pl.BlockSpec((None, SUBLANES, tn), lambda i, j: (i, 0, j)),
        out_shape=jax.ShapeDtypeStruct((DEPTH, SUBLANES, n), F32),
        compiler_params=_cparams(("arbitrary", "arbitrary")),
        name="modulation",
    )(sc, mod_w, mod_b.reshape(DEPTH, 1, n))


def _mod_spec(groups, tile, slot, first_tile=0):
    bounds = []
    for g in groups:
        for b in range(g.nb if g.row0 else 1):
            bounds.append((g.row0 + b * g.length) // tile)

    def index_map(i, *_):
        gidx = 0
        for lo in bounds[1:]:
            gidx = gidx + (i + first_tile >= lo).astype(jnp.int32)
        return (gidx, slot, 0, 0)

    return pl.BlockSpec((None, None, 3, D_MODEL), index_map)


def _two_group_specs(groups, tile, width, axis=0):
    prompt, sample = groups
    n_p = prompt.rows // tile
    n_s = sample.rows // tile
    if axis == 0:
        return [pl.BlockSpec((tile, width), lambda i: (jnp.minimum(i, n_p - 1), 0)),
                pl.BlockSpec((tile, width), lambda i: (jnp.clip(i - n_p, 0, n_s - 1), 0))], n_p
    return [pl.BlockSpec((width, tile), lambda i: (0, jnp.minimum(i, n_p - 1))),
            pl.BlockSpec((width, tile), lambda i: (0, jnp.clip(i - n_p, 0, n_s - 1)))], n_p


def _ffn_kernel(*refs, layer, half, final_norm, n_prompt_tiles):
    if n_prompt_tiles is None:
        x_ref, refs = refs[0], refs[1:]
        load_x = lambda r0, r1: x_ref[r0:r1, :]
    else:
        xp_ref, xs_ref, refs = refs[0], refs[1], refs[2:]
        is_prompt = pl.program_id(0) < n_prompt_tiles
        load_x = lambda r0, r1: jnp.where(is_prompt, xp_ref[r0:r1, :], xs_ref[r0:r1, :])
    mod_ref, g_ref, fg_ref, win_hbm, wout_hbm, o_ref, win_b, wout_b, a_scr, stage_in, stage_out, sem = refs
    n_chunks = D_FF // FF_CHUNK
    n_slots = stage_out.shape[0]
    ahead = n_slots - 1

    def chunk_copies(c):
        lo = c * FF_CHUNK
        slot = c % n_slots
        w_in = win_hbm.at[layer, half]
        return (pltpu.make_async_copy(w_in.at[:, pl.ds(lo, FF_CHUNK)], stage_in.at[slot, 0], sem.at[slot, 0]),
                pltpu.make_async_copy(w_in.at[:, pl.ds(D_FF + lo, FF_CHUNK)], stage_in.at[slot, 1], sem.at[slot, 1]),
                pltpu.make_async_copy(wout_hbm.at[layer, half, pl.ds(lo, FF_CHUNK), :], stage_out.at[slot],
                                      sem.at[slot, 2]))

    m = mod_ref[...]

    def rows_pass(r0, r1, before_chunk=None):
        x = load_x(r0, r1)
        hb = _adaln(x, g_ref[...], m[0:1], m[1:2]).astype(BF16)
        for c in range(n_chunks):
            if before_chunk is not None:
                before_chunk(c)
            lo = c * FF_CHUNK
            gate = jnp.dot(hb, win_b[:, lo:lo + FF_CHUNK], preferred_element_type=F32)
            up = jnp.dot(hb, win_b[:, D_FF + lo:D_FF + lo + FF_CHUNK], preferred_element_type=F32)
            a_scr[r0:r1, lo:lo + FF_CHUNK] = (_silu(gate) * up).astype(BF16)
        y = jnp.dot(a_scr[r0:r1, :], wout_b[...], preferred_element_type=F32)
        out = x + (0.5 * m[2:3]) * y
        if final_norm:
            out = _rmsnorm(out, fg_ref[...])
        o_ref[r0:r1, :] = out

    def land_chunk(c):
        if c + ahead < n_chunks:
            for cp in chunk_copies(c + ahead):
                cp.start()
        for cp in chunk_copies(c):
            cp.wait()
        lo = c * FF_CHUNK
        slot = c % n_slots
        win_b[:, lo:lo + FF_CHUNK] = stage_in[slot, 0].astype(BF16)
        win_b[:, D_FF + lo:D_FF + lo + FF_CHUNK] = stage_in[slot, 1].astype(BF16)
        wout_b[lo:lo + FF_CHUNK, :] = stage_out[slot].astype(BF16)

    tm = o_ref.shape[0]

    @pl.when(pl.program_id(0) == 0)
    def _first_tile():
        for c in range(ahead):
            for cp in chunk_copies(c):
                cp.start()
        rows_pass(0, tm, land_chunk)

    @pl.when(pl.program_id(0) > 0)
    def _other_tiles():
        rows_pass(0, tm // 2)
        rows_pass(tm // 2, tm)


def _ffn_half(x, modt, groups, layer, half, norm_g3, w_in, w_out, final_g, final_norm=False, tiles=None):
    tm = FFN_TILE
    slot = 3 * layer + 2 * half
    if isinstance(x, tuple):
        x_specs, n_prompt_tiles = _two_group_specs(groups, tm, D_MODEL)
        first, count = 0, sum(g.rows for g in groups) // tm
    else:
        first, count = tiles or (0, x.shape[0] // tm)
        x_specs, n_prompt_tiles = [pl.BlockSpec((tm, D_MODEL), lambda i: (i + first, 0))], None
        x = (x,)
    return pl.pallas_call(
        functools.partial(_ffn_kernel, layer=layer, half=half, final_norm=final_norm,
                          n_prompt_tiles=n_prompt_tiles),
        grid=(count,),
        in_specs=x_specs + [
            _mod_spec(groups, tm, slot, first),
            _const_spec((1, D_MODEL), (slot,)),
            _const_spec((1, D_MODEL)),
            pl.BlockSpec(memory_space=pl.ANY),
            pl.BlockSpec(memory_space=pl.ANY),
        ],
        out_specs=pl.BlockSpec((tm, D_MODEL), lambda i: (i, 0)),
        out_shape=jax.ShapeDtypeStruct((count * tm, D_MODEL), F32),
        scratch_shapes=[pltpu.VMEM((D_MODEL, 2 * D_FF), BF16),
                        pltpu.VMEM((D_FF, D_MODEL), BF16),
                        pltpu.VMEM((tm, D_FF), BF16),
                        pltpu.VMEM((FFN_STAGE_SLOTS, 2, D_MODEL, FF_CHUNK), F32),
                        pltpu.VMEM((FFN_STAGE_SLOTS, FF_CHUNK, D_MODEL), F32),
                        pltpu.SemaphoreType.DMA((FFN_STAGE_SLOTS, 3))],
        compiler_params=_cparams(("arbitrary",)),
        name="ffn_half",
    )(*x, modt, norm_g3, final_g.reshape(1, D_MODEL), w_in, w_out)


def _expand_kv(ckv_b, krope, wk_ref, wvt_ref, k_ref, vt_ref):
    kn = jnp.dot(ckv_b, wk_ref[...], preferred_element_type=F32)
    for h in range(MLA_HEADS):
        lo = h * HEAD_PAD
        k_ref[:, lo:lo + HEAD_PAD] = (kn[:, lo:lo + HEAD_PAD] + krope).astype(BF16)
    vt_ref[...] = lax.dot_general(wvt_ref[...], ckv_b, NT, preferred_element_type=F32).astype(BF16)
    ones = jnp.ones((VT_ROWS - V_HEAD, ckv_b.shape[0]), BF16)
    for h in range(MLA_HEADS):
        vt_ref[h * VT_ROWS + V_HEAD:(h + 1) * VT_ROWS, :] = ones


def _mla_proj_kernel(x_ref, mod_ref, g_ref, win_ref, qn_ref, kvn_ref, wq1_ref,
                     wk_ref, wvt_ref, cos_ref, sin_ref,
                     q_ref, k_ref, vt_ref, ckv_ref, kr_ref):
    x = x_ref[...]
    m = mod_ref[...]
    hb = _adaln(x, g_ref[...], m[0:1], m[1:2]).astype(BF16)
    a = jnp.dot(hb, win_ref[...], preferred_element_type=F32)
    qa = _rmsnorm(a[:, :Q_LORA], qn_ref[...]).astype(BF16)
    ckv = _rmsnorm(a[:, Q_LORA:Q_LORA + KV_LORA], kvn_ref[...])
    kr = a[:, Q_LORA + KV_LORA:Q_LORA + KV_LORA + HEAD_PAD]
    kr_rot = a[:, Q_LORA + KV_LORA + HEAD_PAD:]
    cosk = cos_ref[...]
    sink = sin_ref[...]
    lane = lax.broadcasted_iota(jnp.int32, cosk.shape, 1)
    scale = LOG2E / math.sqrt(QK_HEAD)
    cosq = (cosk + (lane < QK_NOPE).astype(F32)) * scale
    q_rot = QK_ROPE // 4
    first = ((lane - QK_NOPE) % (2 * q_rot)) < q_rot
    sin_dn = jnp.where(first, -sink, 0.0) * scale
    sin_up = jnp.where(first, 0.0, sink) * scale
    q1 = jnp.dot(qa, wq1_ref[...], preferred_element_type=F32)
    for h in range(MLA_HEADS):
        lo = h * HEAD_PAD
        qh = q1[:, lo:lo + HEAD_PAD]
        q_ref[:, lo:lo + HEAD_PAD] = (qh * cosq + pltpu.roll(qh, HEAD_PAD - q_rot, 1) * sin_dn
                                      + pltpu.roll(qh, q_rot, 1) * sin_up).astype(BF16)
    ckv_ref[...] = ckv
    kr_ref[...] = kr
    _expand_kv(ckv.astype(BF16), kr * cosk + kr_rot * sink, wk_ref, wvt_ref, k_ref, vt_ref)


def _mla_ctx_kernel(ckv_ref, kr_ref, wk_ref, wvt_ref, k_ref, vt_ref):
    _expand_kv(ckv_ref[...].astype(BF16), kr_ref[...], wk_ref, wvt_ref, k_ref, vt_ref)


def _mla_attn_kernel(*refs, n_parts, heads):
    q_ref = refs[0]
    kv_refs = refs[1:1 + 2 * n_parts]
    o_ref = refs[1 + 2 * n_parts]
    def scores(h):
        qh = q_ref[:, h * HEAD_PAD:(h + 1) * HEAD_PAD]
        return [lax.dot_general(kv_refs[2 * i][:, h * HEAD_PAD:(h + 1) * HEAD_PAD], qh, NT,
                                preferred_element_type=F32) for i in range(n_parts)]

    ss_next = scores(0)
    for h in range(heads):
        ss = ss_next
        if h + 1 < heads:
            ss_next = scores(h + 1)
        mx = ss[0].max(axis=0, keepdims=True)
        for s in ss[1:]:
            mx = jnp.maximum(mx, s.max(axis=0, keepdims=True))
        acc = None
        for i, s in enumerate(ss):
            p = jnp.exp2(s - mx).astype(BF16)
            o = jnp.dot(kv_refs[2 * i + 1][h * VT_ROWS:(h + 1) * VT_ROWS, :], p, preferred_element_type=F32)
            acc = o if acc is None else acc + o
        o_ref[h * V_HEAD:(h + 1) * V_HEAD, :] = (acc[:V_HEAD] / acc[V_HEAD:V_HEAD + 1]).astype(BF16)


def _mla_attention(q, parts, grp, tq, heads):
    nq = grp.length // tq
    qoff = grp.row0 // tq
    in_specs = [pl.BlockSpec((tq, heads * HEAD_PAD), lambda b, p, i: (qoff + b * nq + i, p))]
    args = [q]
    for k, vt, row0, lk in parts:
        koff = row0 // lk
        in_specs.append(pl.BlockSpec((lk, heads * HEAD_PAD), lambda b, p, i, koff=koff: (koff + b, p)))
        in_specs.append(pl.BlockSpec((heads * VT_ROWS, lk), lambda b, p, i, koff=koff: (p, koff + b)))
        args += [k, vt]
    return pl.pallas_call(
        functools.partial(_mla_attn_kernel, n_parts=len(parts), heads=heads),
        grid=(grp.nb, MLA_HEADS // heads, nq),
        in_specs=in_specs,
        out_specs=pl.BlockSpec((heads * V_HEAD, tq), lambda b, p, i: (p, b * nq + i)),
        out_shape=jax.ShapeDtypeStruct((MLA_HEADS * V_HEAD, grp.rows), BF16),
        compiler_params=_cparams(("arbitrary", "arbitrary", "arbitrary")),
        name="mla_attention",
    )(*args)


def _mla_out_kernel(otp_ref, ots_ref, wo_ref, x_ref, mod_ref, o_ref, *, n_prompt_tiles):
    o_t = jnp.where(pl.program_id(0) < n_prompt_tiles, otp_ref[...], ots_ref[...])
    y = lax.dot_general(o_t, wo_ref[...], TN, preferred_element_type=F32)
    o_ref[...] = x_ref[...] + mod_ref[...][2:3] * y


def _rope_tables(groups):
    prompt, sample = groups
    half = QK_ROPE // 2
    freqs = 1.0 / (ROPE_BASE ** (jnp.arange(0, half, 2, dtype=F32) / half))
    pos_i = jnp.arange(sample.length)
    pos = jnp.stack([pos_i // GRID_W, pos_i % GRID_W], axis=-1).astype(F32)
    ang = pos[:, :, None] * freqs
    ang = jnp.broadcast_to(ang[:, :, None, :], (sample.length, 2, 2, half // 2)).reshape(sample.length, QK_ROPE)
    lanes = lambda v: jnp.pad(v, ((0, 0), (QK_NOPE, HEAD_PAD - QK_HEAD)))
    cos = jnp.concatenate([lanes(jnp.ones((prompt.rows, QK_ROPE), F32))] + [lanes(jnp.cos(ang))] * sample.nb, axis=0)
    sin = jnp.concatenate([jnp.zeros((prompt.rows, HEAD_PAD), F32)] + [lanes(jnp.sin(ang))] * sample.nb, axis=0)
    return cos, sin


def _rot_cols(w):
    q = QK_ROPE // 4
    w4 = w.reshape(w.shape[:-1] + (2, 2, q))
    return jnp.stack([-w4[..., 1, :], w4[..., 0, :]], axis=-2).reshape(w.shape)


def _mla_weights(w_in, wq_b, wkv_b):
    d = w_in.shape[0]
    z = lambda *s: jnp.zeros(s, F32)
    k_r = w_in[:, Q_LORA + KV_LORA:]
    pad_tile = lambda c: jnp.concatenate([z(d, QK_NOPE), c, z(d, HEAD_PAD - QK_HEAD)], axis=1)
    w_in_ext = jnp.concatenate([w_in[:, :Q_LORA + KV_LORA], pad_tile(k_r), pad_tile(_rot_cols(k_r))], axis=1)
    wq = wq_b.reshape(Q_LORA, MLA_HEADS, QK_HEAD)
    zq = lambda n: z(Q_LORA, MLA_HEADS, n)
    wq1 = jnp.concatenate([wq, zq(HEAD_PAD - QK_HEAD)], axis=-1)
    wkv = wkv_b.reshape(KV_LORA, MLA_HEADS, QK_NOPE + V_HEAD)
    wk = jnp.concatenate([wkv[..., :QK_NOPE], z(KV_LORA, MLA_HEADS, HEAD_PAD - QK_NOPE)], axis=-1)
    wv = jnp.concatenate([wkv[..., QK_NOPE:], z(KV_LORA, MLA_HEADS, VT_ROWS - V_HEAD)], axis=-1)
    wvt = wv.reshape(KV_LORA, MLA_HEADS * VT_ROWS).T
    flat = lambda w: w.reshape(w.shape[0], -1).astype(BF16)
    return w_in_ext.astype(BF16), flat(wq1), flat(wk), wvt.astype(BF16)


def _mla_layer(x, modt, groups, layer, norm_g3, cache, cos, sin, w_in, q_norm, kv_norm, wq_b, wkv_b, wo):
    t = x.shape[0]
    tm = TOKEN_TILE
    slot = 3 * layer + 1
    w_in_ext, wq1, wk, wvt = _mla_weights(w_in, wq_b, wkv_b)
    hd = MLA_HEADS * HEAD_PAD
    vr = MLA_HEADS * VT_ROWS
    row = lambda c: pl.BlockSpec((tm, c), lambda i: (i, 0))
    q, k, vt, ckv, kr = pl.pallas_call(
        _mla_proj_kernel,
        grid=(t // tm,),
        in_specs=[
            row(D_MODEL), _mod_spec(groups, tm, slot), _const_spec((1, D_MODEL), (slot,)),
            _const_spec(w_in_ext.shape), _const_spec((1, Q_LORA)), _const_spec((1, KV_LORA)),
            _const_spec(wq1.shape), _const_spec(wk.shape), _const_spec(wvt.shape),
            row(HEAD_PAD), row(HEAD_PAD),
        ],
        out_specs=[row(hd), row(hd), pl.BlockSpec((vr, tm), lambda i: (0, i)), row(KV_LORA), row(HEAD_PAD)],
        out_shape=[jax.ShapeDtypeStruct((t, hd), BF16), jax.ShapeDtypeStruct((t, hd), BF16),
                   jax.ShapeDtypeStruct((vr, t), BF16), jax.ShapeDtypeStruct((t, KV_LORA), F32),
                   jax.ShapeDtypeStruct((t, HEAD_PAD), F32)],
        compiler_params=_cparams(("arbitrary",)),
        name="mla_project",
    )(x, modt, norm_g3, w_in_ext, q_norm.reshape(1, -1), kv_norm.reshape(1, -1),
      wq1, wk, wvt, cos, sin)

    prompt, sample = groups
    nb, past, _ = cache.shape
    cflat = cache.reshape(nb * past, -1)
    ckv_c = cflat[:, :KV_LORA]
    kr_c = jnp.pad(cflat[:, KV_LORA:], ((0, 0), (QK_NOPE, HEAD_PAD - QK_HEAD)))
    crow = lambda c: pl.BlockSpec((past, c), lambda i: (i, 0))
    k_c, vt_c = pl.pallas_call(
        _mla_ctx_kernel,
        grid=(nb,),
        in_specs=[crow(KV_LORA), crow(HEAD_PAD), _const_spec(wk.shape), _const_spec(wvt.shape)],
        out_specs=[crow(hd), pl.BlockSpec((vr, past), lambda i: (0, i))],
        out_shape=[jax.ShapeDtypeStruct((nb * past, hd), BF16), jax.ShapeDtypeStruct((vr, nb * past), BF16)],
        compiler_params=_cparams(("arbitrary",)),
        name="mla_ctx_expand",
    )(ckv_c, kr_c, wk, wvt)

    ot_p = _mla_attention(q, [(k, vt, prompt.row0, prompt.length)], prompt, 256, MLA_HEADS)
    ot_s = _mla_attention(q, [(k_c, vt_c, 0, past), (k, vt, sample.row0, sample.length)], sample, 512, 8)
    vd = MLA_HEADS * V_HEAD
    o_specs, n_p = _two_group_specs(groups, tm, vd, axis=1)
    x = pl.pallas_call(
        functools.partial(_mla_out_kernel, n_prompt_tiles=n_p),
        grid=(t // tm,),
        in_specs=o_specs + [_const_spec((vd, D_MODEL)), row(D_MODEL), _mod_spec(groups, tm, slot)],
        out_specs=row(D_MODEL),
        out_shape=jax.ShapeDtypeStruct((t, D_MODEL), F32),
        compiler_params=_cparams(("arbitrary",)),
        name="mla_out",
    )(ot_p, ot_s, wo.astype(BF16), x, modt)
    entry = jnp.concatenate([ckv[:prompt.rows], kr[:prompt.rows, QK_NOPE:QK_HEAD]], axis=-1)
    return x, entry.reshape(prompt.nb, prompt.length, -1)


def _ssm_proj_kernel(x_ref, mod_ref, g_ref, wdt_ref, wt_hbm, z_ref, xbc_ref, dt_ref, wzx_b, stage, sem, *, j):
    rows = stage.shape[1]
    n_slots = stage.shape[0]
    ahead = n_slots - 1
    n_chunks = (D_INNER + CONV_DIM) // rows

    def slab_copy(c):
        return pltpu.make_async_copy(wt_hbm.at[j, pl.ds(c * rows, rows), :], stage.at[c % n_slots],
                                     sem.at[c % n_slots])

    @pl.when(pl.program_id(0) == 0)
    def _load_weights():
        for c in range(ahead):
            slab_copy(c).start()
        for c in range(n_chunks):
            if c + ahead < n_chunks:
                slab_copy(c + ahead).start()
            slab_copy(c).wait()
            for r in range(0, D_MODEL, rows):
                wzx_b[r:r + rows, c * rows:(c + 1) * rows] = stage[c % n_slots, :, r:r + rows].T.astype(BF16)

    m = mod_ref[...]
    hb = _adaln(x_ref[...], g_ref[...], m[0:1], m[1:2]).astype(BF16)
    z_ref[...] = jnp.dot(hb, wzx_b[:, :D_INNER], preferred_element_type=F32).astype(BF16)
    xbc_ref[...] = jnp.dot(hb, wzx_b[:, D_INNER:], preferred_element_type=F32)
    dt_ref[...] = jnp.dot(hb, wdt_ref[...], preferred_element_type=F32)


def _split3(a):
    hi = a.astype(BF16)
    r1 = a - hi.astype(F32)
    mid = r1.astype(BF16)
    lo = (r1 - mid.astype(F32)).astype(BF16)
    return hi, mid, lo


def _ssd_kernel(*refs, zero_init, emit_state, layer, nc):
    it = iter(refs)
    cur_ref, prev_ref, next_ref, dt_ref = next(it), next(it), next(it), next(it)
    cw_ref, cb_ref, dtb_ref, alog_ref, dskip_ref = next(it), next(it), next(it), next(it), next(it)
    h0_ref = None if zero_init else next(it)
    prev_st_ref = next(it) if layer else None
    y_ref = next(it)
    st_ref = next(it) if emit_state else None
    s_f, s_b, sb_loc, y_keep, c_keep, ecb_keep, etb_keep, xpad, act = (next(it) for _ in range(9))

    q = CHUNK
    step = pl.program_id(1)
    npairs = SSM_HEADS // 2
    hpg = SSM_HEADS // SSM_GROUPS
    lane = lax.broadcasted_iota(jnp.int32, (q, LANES), 1)
    left = lane < SSM_HEADDIM

    def pair_cols(v, p):
        return jnp.where(left[:v.shape[0]], v[:, 2 * p:2 * p + 1], v[:, 2 * p + 1:2 * p + 2])

    @pl.when(step == 0)
    def _init():
        if zero_init:
            s_f[...] = jnp.zeros(s_f.shape, F32)
            s_b[...] = jnp.zeros(s_b.shape, F32)
        else:
            s_f[...] = h0_ref[0]
            s_b[...] = h0_ref[1]

    @pl.when(step < nc)
    def _sweep_up():
        c = step
        blk = 4 * SUBLANES
        st = blk // SUBLANES
        has_prev = c > 0
        has_next = c < nc - 1
        for j in range(CONV_DIM // LANES):
            cols = slice(j * LANES, (j + 1) * LANES)
            xpad[j, 0:SUBLANES, :] = jnp.where(has_prev, prev_ref[:, cols], 0.0)
            xpad[j, SUBLANES:SUBLANES + q, :] = cur_ref[:, cols]
            xpad[j, SUBLANES + q:, :] = jnp.where(has_next, next_ref[:, cols], 0.0)
            wk = [jnp.broadcast_to(cw_ref[k:k + 1, cols], (SUBLANES, LANES)) for k in range(CONV_W)]
            bias = jnp.broadcast_to(cb_ref[:, cols], (SUBLANES, LANES))
            for b in range(q // blk):
                r0 = SUBLANES + b * blk - CONV_W // 2
                taps = [xpad[j, pl.ds(r0 + u, SUBLANES, stride=st), :] for u in range(st + CONV_W - 1)]
                for v in range(st):
                    acc = bias
                    for k in range(CONV_W):
                        acc = acc + taps[v + k] * wk[k]
                    act[j, pl.ds(b * blk + v, SUBLANES, stride=st), :] = _silu(acc)
        n_x = D_INNER // LANES
        for g in range(SSM_GROUPS):
            c_keep[c, :, g * D_STATE:(g + 1) * D_STATE] = act[n_x + SSM_GROUPS + g].astype(BF16)

        dtr = dt_ref[...] + dtb_ref[...]
        dtv = jnp.maximum(dtr, 0.0) + jnp.log1p(jnp.exp(-jnp.abs(dtr)))
        a2 = dtv * (-LOG2E * jnp.exp(alog_ref[...]))
        ri = lax.broadcasted_iota(jnp.int32, (q, q), 0)
        ci = lax.broadcasted_iota(jnp.int32, (q, q), 1)
        lower = ci <= ri
        slower = ci < ri
        supper = ci > ri
        tot = jnp.sum(a2, axis=0, keepdims=True)
        dirs = []
        for d, keep in enumerate((lower, ci >= ri)):
            ad = a2[:, d * LANES:(d + 1) * LANES]
            dtd = dtv[:, d * LANES:(d + 1) * LANES]
            td = tot[:, d * LANES:(d + 1) * LANES]
            tri = keep.astype(BF16)
            cum = sum(jnp.dot(tri, part, preferred_element_type=F32) for part in _split3(ad))
            dirs.append(dict(cum=cum, cum_t=cum.T, dt_t=dtd.T, ecum=jnp.exp2(cum), etot=jnp.exp2(td),
                             wout_t=(jnp.exp2(td - cum) * dtd).T))
        fw, bw = dirs
        dt_sum_t = fw["dt_t"] + bw["dt_t"]
        ecb_keep[c] = bw["ecum"]
        etb_keep[c] = jnp.broadcast_to(bw["etot"], (SUBLANES, LANES))
        dsum = dskip_ref[0:1, :] + dskip_ref[1:2, :]

        for g in range(SSM_GROUPS):
            bm = act[n_x + g]
            cm_b = c_keep[c, :, g * D_STATE:(g + 1) * D_STATE]
            bm_t = bm.T
            cb = lax.dot_general(cm_b, bm.astype(BF16), NT, preferred_element_type=F32)
            for pp in range(hpg // 2):
                p = g * (hpg // 2) + pp
                xp = act[p]
                xbd = jnp.concatenate([jnp.where(left, xp, 0.0), jnp.where(left, 0.0, xp)], axis=0).astype(BF16)
                ms, bfs, bbs = [], [], []
                for h in (2 * p, 2 * p + 1):
                    seg = jnp.where(lower, fw["cum"][:, h:h + 1] - fw["cum_t"][h:h + 1, :],
                                    bw["cum"][:, h:h + 1] - bw["cum_t"][h:h + 1, :])
                    wgt = jnp.where(slower, fw["dt_t"][h:h + 1, :],
                                    jnp.where(supper, bw["dt_t"][h:h + 1, :], dt_sum_t[h:h + 1, :]))
                    ms.append((cb * jnp.exp2(seg) * wgt).astype(BF16))
                    bfs.append((bm_t * fw["wout_t"][h:h + 1, :]).astype(BF16))
                    bbs.append((bm_t * bw["wout_t"][h:h + 1, :]).astype(BF16))
                y = jnp.dot(jnp.concatenate(ms, axis=1), xbd, preferred_element_type=F32)
                st = s_f[p]
                y_off = jnp.dot(cm_b, st.astype(BF16), preferred_element_type=F32)
                y_keep[c, :, p * LANES:(p + 1) * LANES] = (y + y_off * pair_cols(fw["ecum"], p)
                                                        + dsum[:, p * LANES:(p + 1) * LANES] * xp)
                s_f[p] = st * pair_cols(fw["etot"], p) + jnp.dot(jnp.concatenate(bfs, axis=1), xbd,
                                                                 preferred_element_type=F32)
                sb_loc[c, p] = jnp.dot(jnp.concatenate(bbs, axis=1), xbd, preferred_element_type=F32)

    @pl.when(step >= nc)
    def _sweep_down():
        c = 2 * nc - 1 - step
        ecb = ecb_keep[c]
        etb = etb_keep[c][0:1]
        for p in range(npairs):
            g = p // (hpg // 2)
            st = s_b[p]
            y_off = jnp.dot(c_keep[c, :, g * D_STATE:(g + 1) * D_STATE], st.astype(BF16),
                            preferred_element_type=F32)
            y_ref[:, p * LANES:(p + 1) * LANES] = (y_keep[c, :, p * LANES:(p + 1) * LANES]
                                                    + y_off * pair_cols(ecb, p))
            s_b[p] = st * pair_cols(etb, p) + sb_loc[c, p]

        if emit_state:
            @pl.when(step == 2 * nc - 1)
            def _emit():
                for l in range(layer):
                    st_ref[l] = prev_st_ref[l]
                for p in range(npairs):
                    st_ref[layer, 0, p] = s_f[p].T
                    st_ref[layer, 1, p] = s_b[p].T


def _ssd_scan(xbc, dt, grp, conv_w, conv_b, dt_bias, a_log, d_skip, h0, state_out):
    t = xbc.shape[0]
    q = CHUNK
    nc = grp.length // q
    zero_init = h0 is None
    emit_state = state_out is not None
    boff = grp.row0 // q
    npairs = SSM_HEADS // 2
    qs = q // SUBLANES

    chunk = lambda b, s: boff + b * nc + jnp.minimum(s, nc - 1)
    in_specs = [
        pl.BlockSpec((q, CONV_DIM), lambda b, s: (chunk(b, s), 0)),
        pl.BlockSpec((SUBLANES, CONV_DIM), lambda b, s: (jnp.maximum(chunk(b, s) * qs - 1, 0), 0)),
        pl.BlockSpec((SUBLANES, CONV_DIM), lambda b, s: (jnp.minimum((chunk(b, s) + 1) * qs, t // SUBLANES - 1), 0)),
        pl.BlockSpec((q, 2 * LANES), lambda b, s: (chunk(b, s), 0)),
        _const_spec((SUBLANES, CONV_DIM)), _const_spec((1, CONV_DIM)),
        _const_spec((1, 2 * LANES)), _const_spec((1, 2 * LANES)), _const_spec((2, D_INNER)),
    ]
    args = [xbc, xbc, xbc, dt, conv_w, conv_b, dt_bias, a_log, d_skip]
    if not zero_init:
        in_specs.append(pl.BlockSpec((None, 2, npairs, D_STATE, LANES), lambda b, s: (b, 0, 0, 0, 0)))
        args.append(h0)
    state_block = lambda n: pl.BlockSpec((None, n, 2, npairs, LANES, D_STATE), lambda b, s: (b, 0, 0, 0, 0, 0))
    layer = state_out[1] if emit_state else 0
    if layer:
        in_specs.append(state_block(layer))
        args.append(state_out[0])
    out_specs = [
        pl.BlockSpec((q, D_INNER), lambda b, s: (b * nc + jnp.where(s < nc, nc - 1, 2 * nc - 1 - s), 0)),
    ]
    out_shape = [jax.ShapeDtypeStruct((grp.rows, D_INNER), F32)]
    if emit_state:
        out_specs.append(state_block(layer + 1))
        out_shape.append(jax.ShapeDtypeStruct((grp.nb, layer + 1, 2, npairs, LANES, D_STATE), F32))
    return pl.pallas_call(
        functools.partial(_ssd_kernel, zero_init=zero_init, emit_state=emit_state, layer=layer, nc=nc),
        grid=(grp.nb, 2 * nc),
        in_specs=in_specs,
        out_specs=out_specs,
        out_shape=out_shape,
        scratch_shapes=[pltpu.VMEM((npairs, D_STATE, LANES), F32),
                        pltpu.VMEM((npairs, D_STATE, LANES), F32),
                        pltpu.VMEM((nc, npairs, D_STATE, LANES), F32),
                        pltpu.VMEM((nc, q, D_INNER), F32),
                        pltpu.VMEM((nc, q, SSM_GROUPS * D_STATE), BF16),
                        pltpu.VMEM((nc, q, LANES), F32),
                        pltpu.VMEM((nc, SUBLANES, LANES), F32),
                        pltpu.VMEM((CONV_DIM // LANES, q + 2 * SUBLANES, LANES), F32),
                        pltpu.VMEM((CONV_DIM // LANES, q, LANES), F32)],
        compiler_params=_cparams(("arbitrary", "arbitrary")),
        name="ssd_scan",
    )(*args)


def _ssm_out_kernel(yp_ref, ys_ref, z_ref, ng_ref, w_ref, x_ref, mod_ref, o_ref, *, n_prompt_tiles):
    y = jnp.where(pl.program_id(0) < n_prompt_tiles, yp_ref[...], ys_ref[...])
    y = y * _silu(z_ref[...].astype(F32))
    yn = _rmsnorm(y, ng_ref[...]).astype(BF16)
    o_ref[...] = x_ref[...] + mod_ref[...][2:3] * jnp.dot(yn, w_ref[...], preferred_element_type=F32)


def _pair_state(h):
    b = h.shape[0]
    return h.reshape(b, 2, SSM_HEADS // 2, 2, SSM_HEADDIM, D_STATE).transpose(0, 1, 2, 5, 3, 4).reshape(
        b, 2, SSM_HEADS // 2, D_STATE, 2 * SSM_HEADDIM)


def _ssm_layer(x, modt, groups, layer, norm_g3, h0_sample, state_out, weights, conv_w, conv_b,
               dt_bias, a_log, d_skip, norm_g):
    j, w_in, w_dt, w_out = weights
    t = x.shape[0]
    tm = TOKEN_TILE
    slot = 3 * layer + 1
    row = lambda c: pl.BlockSpec((tm, c), lambda i: (i, 0))
    z, xbc, dt = pl.pallas_call(
        functools.partial(_ssm_proj_kernel, j=j),
        grid=(t // tm,),
        in_specs=[row(D_MODEL), _mod_spec(groups, tm, slot), _const_spec((1, D_MODEL), (slot,)),
                  _const_spec(w_dt.shape[1:], (j,)), pl.BlockSpec(memory_space=pl.ANY)],
        out_specs=[row(D_INNER), row(CONV_DIM), row(2 * LANES)],
        out_shape=[jax.ShapeDtypeStruct((t, D_INNER), BF16), jax.ShapeDtypeStruct((t, CONV_DIM), F32),
                   jax.ShapeDtypeStruct((t, 2 * LANES), F32)],
        scratch_shapes=[pltpu.VMEM((D_MODEL, D_INNER + CONV_DIM), BF16),
                        pltpu.VMEM((SSM_STAGE_SLOTS, SSM_STAGE_ROWS, D_MODEL), F32),
                        pltpu.SemaphoreType.DMA((SSM_STAGE_SLOTS,))],
        compiler_params=_cparams(("arbitrary",)),
        name="ssm_project",
    )(x, modt, norm_g3, w_dt, w_in)

    lane_pad = lambda v: jnp.pad(v, ((0, 0), (0, LANES - v.shape[-1]))).reshape(1, 2 * LANES)
    cw = jnp.pad(conv_w, ((0, SUBLANES - CONV_W), (0, 0)))
    cb = conv_b.reshape(1, CONV_DIM)
    dsk = jnp.repeat(d_skip, SSM_HEADDIM, axis=-1)
    prompt, sample = groups
    scan = functools.partial(_ssd_scan, xbc, dt)
    yp, new_state = scan(prompt, cw, cb, lane_pad(dt_bias), lane_pad(a_log), dsk, None, state_out)
    ys, = scan(sample, cw, cb, lane_pad(dt_bias), lane_pad(a_log), dsk, _pair_state(h0_sample), None)

    to = TOKEN_TILE
    row = lambda c: pl.BlockSpec((to, c), lambda i: (i, 0))
    y_specs, n_p = _two_group_specs(groups, to, D_INNER)
    x = pl.pallas_call(
        functools.partial(_ssm_out_kernel, n_prompt_tiles=n_p),
        grid=(t // to,),
        in_specs=y_specs + [row(D_INNER), _const_spec((1, D_INNER)),
                  _const_spec((D_INNER, D_MODEL), (j,)), row(D_MODEL), _mod_spec(groups, to, slot)],
        out_specs=row(D_MODEL),
        out_shape=jax.ShapeDtypeStruct((t, D_MODEL), F32),
        compiler_params=_cparams(("arbitrary",)),
        name="ssm_out",
    )(yp, ys, z, norm_g.reshape(1, D_INNER), w_out, x, modt)
    return x, new_state


def kernel(x_prompt, x_sample, cache_mla, state_ssm, c, c_ctx, mod_w, mod_b, norm_g, ffn_w_in, ffn_w_out,
           mla_w_in, mla_q_norm, mla_kv_norm, mla_wq_b, mla_wkv_b, mla_wo, ssm_w_in, ssm_conv_w, ssm_conv_b,
           ssm_dt_bias, ssm_a_log, ssm_d, ssm_norm_g, ssm_w_out, final_norm_g):
    nbp, lp, d = x_prompt.shape
    nbs, ls, _ = x_sample.shape
    prompt = _Group(0, nbp, lp)
    sample = _Group(nbp * lp, nbs, ls)
    groups = (prompt, sample)
    x = (x_prompt.reshape(-1, d), x_sample.reshape(-1, d))

    ncond = 1 + nbs
    sc = jnp.concatenate([c_ctx[None, :], c, jnp.zeros((SUBLANES - ncond, d), F32)], axis=0)
    mod = _modulation(sc, mod_w, mod_b)
    modt = mod[:, :ncond].reshape(DEPTH, ncond, 3, 3, d).transpose(1, 0, 2, 3, 4).reshape(ncond, DEPTH * 3, 3, d)

    norm_g3 = norm_g.reshape(DEPTH * 3, 1, d)
    ffn_in = ffn_w_in
    ffn_out = ffn_w_out
    n_ssm = ssm_w_in.shape[0]
    ssm_w_in_t = jnp.swapaxes(ssm_w_in, 1, 2)
    wdt = ssm_w_in[:, :, D_INNER + CONV_DIM:].reshape(n_ssm, d, 2, SSM_HEADS)
    ssm_wdt = jnp.pad(wdt, ((0, 0), (0, 0), (0, 0), (0, LANES - SSM_HEADS))).reshape(n_ssm, d, 2 * LANES).astype(BF16)
    ssm_wo = ssm_w_out.astype(BF16)

    cos, sin = _rope_tables(groups)
    new_mla = []
    new_state = None
    tiles_p = prompt.rows // FFN_TILE
    tiles_s = sample.rows // FFN_TILE
    for i in range(DEPTH):
        j = i // 2
        x = _ffn_half(x, modt, groups, i, 0, norm_g3, ffn_in, ffn_out, final_norm_g)
        if i % 2 == 0:
            x, entry = _mla_layer(x, modt, groups, i, norm_g3, cache_mla[:, j], cos, sin,
                                  mla_w_in[j], mla_q_norm[j], mla_kv_norm[j], mla_wq_b[j], mla_wkv_b[j], mla_wo[j])
            new_mla.append(entry)
        else:
            x, new_state = _ssm_layer(x, modt, groups, i, norm_g3, state_ssm[:, j], (new_state, j),
                                      (j, ssm_w_in_t, ssm_wdt, ssm_wo), ssm_conv_w[j], ssm_conv_b[j],
                                      ssm_dt_bias[j], ssm_a_log[j], ssm_d[j], ssm_norm_g[j])
        if i < DEPTH - 1:
            x = _ffn_half(x, modt, groups, i, 1, norm_g3, ffn_in, ffn_out, final_norm_g)
    last = functools.partial(_ffn_half, x, modt, groups, DEPTH - 1, 1, norm_g3, ffn_in, ffn_out, final_norm_g, True)
    y_prompt = last((0, tiles_p)).reshape(nbp, lp, d)
    y_sample = last((tiles_p, tiles_s)).reshape(nbs, ls, d)
    new_state = new_state.reshape(nbp, n_ssm, 2, SSM_HEADS, SSM_HEADDIM, D_STATE)
    return y_prompt, y_sample, jnp.stack(new_mla, axis=1), new_state
```

```python
import functools
import math

import jax
import jax.numpy as jnp
from jax import lax
from jax.experimental import pallas as pl
from jax.experimental.pallas import tpu as pltpu

F32 = jnp.float32
BF16 = jnp.bfloat16

D_MODEL = 1024
DEPTH = 4
N_MOD = 9
D_FF = 2816
GRID_W = 64
MLA_HEADS = 16
Q_LORA = 512
KV_LORA = 256
QK_NOPE = 64
QK_ROPE = 32
V_HEAD = 64
QK_HEAD = QK_NOPE + QK_ROPE
ROPE_BASE = 10000.0
HEAD_PAD = 128
VT_ROWS = 80
D_INNER = 2048
SSM_HEADDIM = 64
SSM_HEADS = 32
SSM_GROUPS = 4
D_STATE = 128
CONV_W = 5
CONV_DIM = D_INNER + 2 * SSM_GROUPS * D_STATE
CHUNK = 128
EPS = 1e-6
LOG2E = math.log2(math.e)

LANES = 128
SUBLANES = 8
TOKEN_TILE = 512
FFN_TILE = 512
FF_CHUNK = 256
FFN_STAGE_SLOTS = 4
SSM_STAGE_SLOTS = 4
SSM_STAGE_ROWS = 128
VMEM_LIMIT = 56 * 1024 * 1024

NT = (((1,), (1,)), ((), ()))
TN = (((0,), (0,)), ((), ()))


def _cparams(sem):
    return pltpu.CompilerParams(dimension_semantics=sem, vmem_limit_bytes=VMEM_LIMIT)


def _const_spec(shape, index=None):
    index = tuple(index or ())
    block = (None,) * len(index) + tuple(shape)
    zeros = (0,) * len(shape)
    return pl.BlockSpec(block, lambda *_: index + zeros, pipeline_mode=pl.Buffered(1))


def _rmsnorm(x, g):
    ms = jnp.mean(x * x, axis=-1, keepdims=True)
    return (x * lax.rsqrt(ms + EPS)) * g


def _silu(x):
    h = 0.5 * x
    return h * jnp.tanh(h) + h


def _adaln(x, g, shift, scale):
    return _rmsnorm(x, g) * (1.0 + scale) + shift


class _Group:
    def __init__(self, row0, nb, length):
        self.row0, self.nb, self.length = row0, nb, length

    @property
    def rows(self):
        return self.nb * self.length


def _mod_kernel(sc_ref, w_ref, b_ref, o_ref):
    s = _silu(sc_ref[...])
    o_ref[...] = jnp.dot(s.astype(BF16), w_ref[...].astype(BF16),
                         preferred_element_type=F32) + b_ref[...]


def _modulation(sc, mod_w, mod_b):
    tn = 2304
    n = N_MOD * D_MODEL
    return pl.pallas_call(
        _mod_kernel,
        grid=(DEPTH, n // tn),
        in_specs=[
            pl.BlockSpec((SUBLANES, D_MODEL), lambda i, j: (0, 0)),
            pl.BlockSpec((None, D_MODEL, tn), lambda i, j: (i, 0, j)),
            pl.BlockSpec((None, 1, tn), lambda i, j: (i, 0, j)),
        ],
        out_specs=pl.BlockSpec((None, SUBLANES, tn), lambda i, j: (i, 0, j)),
        out_shape=jax.ShapeDtypeStruct((DEPTH, SUBLANES, n), F32),
        compiler_params=_cparams(("arbitrary", "arbitrary")),
        name="modulation",
    )(sc, mod_w, mod_b.reshape(DEPTH, 1, n))


def _mod_spec(groups, tile, slot, first_tile=0):
    bounds = []
    for g in groups:
        for b in range(g.nb if g.row0 else 1):
            bounds.append((g.row0 + b * g.length) // tile)

    def index_map(i, *_):
        gidx = 0
        for lo in bounds[1:]:
            gidx = gidx + (i + first_tile >= lo).astype(jnp.int32)
        return (gidx, slot, 0, 0)

    return pl.BlockSpec((None, None, 3, D_MODEL), index_map)


def _two_group_specs(groups, tile, width, axis=0):
    prompt, sample = groups
    n_p = prompt.rows // tile
    n_s = sample.rows // tile
    if axis == 0:
        return [pl.BlockSpec((tile, width), lambda i: (jnp.minimum(i, n_p - 1), 0)),
                pl.BlockSpec((tile, width), lambda i: (jnp.clip(i - n_p, 0, n_s - 1), 0))], n_p
    return [pl.BlockSpec((width, tile), lambda i: (0, jnp.minimum(i, n_p - 1))),
            pl.BlockSpec((width, tile), lambda i: (0, jnp.clip(i - n_p, 0, n_s - 1)))], n_p


def _ffn_kernel(*refs, layer, half, final_norm, n_prompt_tiles):
    if n_prompt_tiles is None:
        x_ref, refs = refs[0], refs[1:]
        load_x = lambda r0, r1: x_ref[r0:r1, :]
    else:
        xp_ref, xs_ref, refs = refs[0], refs[1], refs[2:]
        is_prompt = pl.program_id(0) < n_prompt_tiles
        load_x = lambda r0, r1: jnp.where(is_prompt, xp_ref[r0:r1, :], xs_ref[r0:r1, :])
    mod_ref, g_ref, fg_ref, win_hbm, wout_hbm, o_ref, win_b, wout_b, a_scr, stage_in, stage_out, sem = refs
    n_chunks = D_FF // FF_CHUNK
    n_slots = stage_out.shape[0]
    ahead = n_slots - 1

    def chunk_copies(c):
        lo = c * FF_CHUNK
        slot = c % n_slots
        w_in = win_hbm.at[layer, half]
        return (pltpu.make_async_copy(w_in.at[:, pl.ds(lo, FF_CHUNK)], stage_in.at[slot, 0], sem.at[slot, 0]),
                pltpu.make_async_copy(w_in.at[:, pl.ds(D_FF + lo, FF_CHUNK)], stage_in.at[slot, 1], sem.at[slot, 1]),
                pltpu.make_async_copy(wout_hbm.at[layer, half, pl.ds(lo, FF_CHUNK), :], stage_out.at[slot],
                                      sem.at[slot, 2]))

    m = mod_ref[...]

    def rows_pass(r0, r1, before_chunk=None):
        x = load_x(r0, r1)
        hb = _adaln(x, g_ref[...], m[0:1], m[1:2]).astype(BF16)
        for c in range(n_chunks):
            if before_chunk is not None:
                before_chunk(c)
            lo = c * FF_CHUNK
            gate = jnp.dot(hb, win_b[:, lo:lo + FF_CHUNK], preferred_element_type=F32)
            up = jnp.dot(hb, win_b[:, D_FF + lo:D_FF + lo + FF_CHUNK], preferred_element_type=F32)
            a_scr[r0:r1, lo:lo + FF_CHUNK] = (_silu(gate) * up).astype(BF16)
        y = jnp.dot(a_scr[r0:r1, :], wout_b[...], preferred_element_type=F32)
        out = x + (0.5 * m[2:3]) * y
        if final_norm:
            out = _rmsnorm(out, fg_ref[...])
        o_ref[r0:r1, :] = out

    def land_chunk(c):
        if c + ahead < n_chunks:
            for k, cp in enumerate(chunk_copies(c + ahead)):
                cp.start(priority=k % 2)
        for cp in chunk_copies(c):
            cp.wait()
        lo = c * FF_CHUNK
        slot = c % n_slots
        win_b[:, lo:lo + FF_CHUNK] = stage_in[slot, 0].astype(BF16)
        win_b[:, D_FF + lo:D_FF + lo + FF_CHUNK] = stage_in[slot, 1].astype(BF16)
        wout_b[lo:lo + FF_CHUNK, :] = stage_out[slot].astype(BF16)

    tm = o_ref.shape[0]

    @pl.when(pl.program_id(0) == 0)
    def _first_tile():
        for c in range(ahead):
            for k, cp in enumerate(chunk_copies(c)):
                cp.start(priority=k % 2)
        rows_pass(0, tm, land_chunk)

    @pl.when(pl.program_id(0) > 0)
    def _other_tiles():
        rows_pass(0, tm // 2)
        rows_pass(tm // 2, tm)


def _ffn_half(x, modt, groups, layer, half, norm_g3, w_in, w_out, final_g, final_norm=False, tiles=None):
    tm = FFN_TILE
    slot = 3 * layer + 2 * half
    if isinstance(x, tuple):
        x_specs, n_prompt_tiles = _two_group_specs(groups, tm, D_MODEL)
        first, count = 0, sum(g.rows for g in groups) // tm
    else:
        first, count = tiles or (0, x.shape[0] // tm)
        x_specs, n_prompt_tiles = [pl.BlockSpec((tm, D_MODEL), lambda i: (i + first, 0))], None
        x = (x,)
    return pl.pallas_call(
        functools.partial(_ffn_kernel, layer=layer, half=half, final_norm=final_norm,
                          n_prompt_tiles=n_prompt_tiles),
        grid=(count,),
        in_specs=x_specs + [
            _mod_spec(groups, tm, slot, first),
            _const_spec((1, D_MODEL), (slot,)),
            _const_spec((1, D_MODEL)),
            pl.BlockSpec(memory_space=pl.ANY),
            pl.BlockSpec(memory_space=pl.ANY),
        ],
        out_specs=pl.BlockSpec((tm, D_MODEL), lambda i: (i, 0)),
        out_shape=jax.ShapeDtypeStruct((count * tm, D_MODEL), F32),
        scratch_shapes=[pltpu.VMEM((D_MODEL, 2 * D_FF), BF16),
                        pltpu.VMEM((D_FF, D_MODEL), BF16),
                        pltpu.VMEM((tm, D_FF), BF16),
                        pltpu.VMEM((FFN_STAGE_SLOTS, 2, D_MODEL, FF_CHUNK), F32),
                        pltpu.VMEM((FFN_STAGE_SLOTS, FF_CHUNK, D_MODEL), F32),
                        pltpu.SemaphoreType.DMA((FFN_STAGE_SLOTS, 3))],
        compiler_params=_cparams(("arbitrary",)),
        name="ffn_half",
    )(*x, modt, norm_g3, final_g.reshape(1, D_MODEL), w_in, w_out)


def _expand_kv(ckv_b, krope, wk_ref, wvt_ref, k_ref, vt_ref):
    kn = jnp.dot(ckv_b, wk_ref[...], preferred_element_type=F32)
    for h in range(MLA_HEADS):
        lo = h * HEAD_PAD
        k_ref[:, lo:lo + HEAD_PAD] = (kn[:, lo:lo + HEAD_PAD] + krope).astype(BF16)
    vt_ref[...] = lax.dot_general(wvt_ref[...], ckv_b, NT, preferred_element_type=F32).astype(BF16)
    ones = jnp.ones((VT_ROWS - V_HEAD, ckv_b.shape[0]), BF16)
    for h in range(MLA_HEADS):
        vt_ref[h * VT_ROWS + V_HEAD:(h + 1) * VT_ROWS, :] = ones


def _mla_proj_kernel(x_ref, mod_ref, g_ref, win_ref, qn_ref, kvn_ref, wq1_ref,
                     wk_ref, wvt_ref, cos_ref, sin_ref,
                     q_ref, k_ref, vt_ref, ckv_ref, kr_ref):
    x = x_ref[...]
    m = mod_ref[...]
    hb = _adaln(x, g_ref[...], m[0:1], m[1:2]).astype(BF16)
    a = jnp.dot(hb, win_ref[...], preferred_element_type=F32)
    qa = _rmsnorm(a[:, :Q_LORA], qn_ref[...]).astype(BF16)
    ckv = _rmsnorm(a[:, Q_LORA:Q_LORA + KV_LORA], kvn_ref[...])
    kr = a[:, Q_LORA + KV_LORA:Q_LORA + KV_LORA + HEAD_PAD]
    kr_rot = a[:, Q_LORA + KV_LORA + HEAD_PAD:]
    cosk = cos_ref[...]
    sink = sin_ref[...]
    lane = lax.broadcasted_iota(jnp.int32, cosk.shape, 1)
    scale = LOG2E / math.sqrt(QK_HEAD)
    cosq = (cosk + (lane < QK_NOPE).astype(F32)) * scale
    q_rot = QK_ROPE // 4
    first = ((lane - QK_NOPE) % (2 * q_rot)) < q_rot
    sin_dn = jnp.where(first, -sink, 0.0) * scale
    sin_up = jnp.where(first, 0.0, sink) * scale
    q1 = jnp.dot(qa, wq1_ref[...], preferred_element_type=F32)
    for h in range(MLA_HEADS):
        lo = h * HEAD_PAD
        qh = q1[:, lo:lo + HEAD_PAD]
        q_ref[:, lo:lo + HEAD_PAD] = (qh * cosq + pltpu.roll(qh, HEAD_PAD - q_rot, 1) * sin_dn
                                      + pltpu.roll(qh, q_rot, 1) * sin_up).astype(BF16)
    ckv_ref[...] = ckv
    kr_ref[...] = kr
    _expand_kv(ckv.astype(BF16), kr * cosk + kr_rot * sink, wk_ref, wvt_ref, k_ref, vt_ref)


def _mla_ctx_kernel(ckv_ref, kr_ref, wk_ref, wvt_ref, k_ref, vt_ref):
    _expand_kv(ckv_ref[...].astype(BF16), kr_ref[...], wk_ref, wvt_ref, k_ref, vt_ref)


def _mla_attn_kernel(*refs, n_parts, heads):
    q_ref = refs[0]
    kv_refs = refs[1:1 + 2 * n_parts]
    o_ref = refs[1 + 2 * n_parts]
    def scores(h):
        qh = q_ref[:, h * HEAD_PAD:(h + 1) * HEAD_PAD]
        return [lax.dot_general(kv_refs[2 * i][:, h * HEAD_PAD:(h + 1) * HEAD_PAD], qh, NT,
                                preferred_element_type=F32) for i in range(n_parts)]

    ss_next = scores(0)
    for h in range(heads):
        ss = ss_next
        if h + 1 < heads:
            ss_next = scores(h + 1)
        mx = ss[0].max(axis=0, keepdims=True)
        for s in ss[1:]:
            mx = jnp.maximum(mx, s.max(axis=0, keepdims=True))
        acc = None
        for i, s in enumerate(ss):
            p = jnp.exp2(s - mx).astype(BF16)
            o = jnp.dot(kv_refs[2 * i + 1][h * VT_ROWS:(h + 1) * VT_ROWS, :], p, preferred_element_type=F32)
            acc = o if acc is None else acc + o
        o_ref[h * V_HEAD:(h + 1) * V_HEAD, :] = (acc[:V_HEAD] / acc[V_HEAD:V_HEAD + 1]).astype(BF16)


def _mla_attention(q, parts, grp, tq, heads):
    nq = grp.length // tq
    qoff = grp.row0 // tq
    in_specs = [pl.BlockSpec((tq, heads * HEAD_PAD), lambda b, p, i: (qoff + b * nq + i, p))]
    args = [q]
    for k, vt, row0, lk in parts:
        koff = row0 // lk
        in_specs.append(pl.BlockSpec((lk, heads * HEAD_PAD), lambda b, p, i, koff=koff: (koff + b, p)))
        in_specs.append(pl.BlockSpec((heads * VT_ROWS, lk), lambda b, p, i, koff=koff: (p, koff + b)))
        args += [k, vt]
    return pl.pallas_call(
        functools.partial(_mla_attn_kernel, n_parts=len(parts), heads=heads),
        grid=(grp.nb, MLA_HEADS // heads, nq),
        in_specs=in_specs,
        out_specs=pl.BlockSpec((heads * V_HEAD, tq), lambda b, p, i: (p, b * nq + i)),
        out_shape=jax.ShapeDtypeStruct((MLA_HEADS * V_HEAD, grp.rows), BF16),
        compiler_params=_cparams(("arbitrary", "arbitrary", "arbitrary")),
        name="mla_attention",
    )(*args)


def _mla_out_kernel(otp_ref, ots_ref, wo_ref, x_ref, mod_ref, o_ref, *, n_prompt_tiles):
    o_t = jnp.where(pl.program_id(0) < n_prompt_tiles, otp_ref[...], ots_ref[...])
    y = lax.dot_general(o_t, wo_ref[...], TN, preferred_element_type=F32)
    o_ref[...] = x_ref[...] + mod_ref[...][2:3] * y


def _rope_tables(groups):
    prompt, sample = groups
    half = QK_ROPE // 2
    freqs = 1.0 / (ROPE_BASE ** (jnp.arange(0, half, 2, dtype=F32) / half))
    pos_i = jnp.arange(sample.length)
    pos = jnp.stack([pos_i // GRID_W, pos_i % GRID_W], axis=-1).astype(F32)
    ang = pos[:, :, None] * freqs
    ang = jnp.broadcast_to(ang[:, :, None, :], (sample.length, 2, 2, half // 2)).reshape(sample.length, QK_ROPE)
    lanes = lambda v: jnp.pad(v, ((0, 0), (QK_NOPE, HEAD_PAD - QK_HEAD)))
    cos = jnp.concatenate([lanes(jnp.ones((prompt.rows, QK_ROPE), F32))] + [lanes(jnp.cos(ang))] * sample.nb, axis=0)
    sin = jnp.concatenate([jnp.zeros((prompt.rows, HEAD_PAD), F32)] + [lanes(jnp.sin(ang))] * sample.nb, axis=0)
    return cos, sin


def _rot_cols(w):
    q = QK_ROPE // 4
    w4 = w.reshape(w.shape[:-1] + (2, 2, q))
    return jnp.stack([-w4[..., 1, :], w4[..., 0, :]], axis=-2).reshape(w.shape)


def _mla_weights(w_in, wq_b, wkv_b):
    d = w_in.shape[0]
    z = lambda *s: jnp.zeros(s, F32)
    k_r = w_in[:, Q_LORA + KV_LORA:]
    pad_tile = lambda c: jnp.concatenate([z(d, QK_NOPE), c, z(d, HEAD_PAD - QK_HEAD)], axis=1)
    w_in_ext = jnp.concatenate([w_in[:, :Q_LORA + KV_LORA], pad_tile(k_r), pad_tile(_rot_cols(k_r))], axis=1)
    wq = wq_b.reshape(Q_LORA, MLA_HEADS, QK_HEAD)
    zq = lambda n: z(Q_LORA, MLA_HEADS, n)
    wq1 = jnp.concatenate([wq, zq(HEAD_PAD - QK_HEAD)], axis=-1)
    wkv = wkv_b.reshape(KV_LORA, MLA_HEADS, QK_NOPE + V_HEAD)
    wk = jnp.concatenate([wkv[..., :QK_NOPE], z(KV_LORA, MLA_HEADS, HEAD_PAD - QK_NOPE)], axis=-1)
    wv = jnp.concatenate([wkv[..., QK_NOPE:], z(KV_LORA, MLA_HEADS, VT_ROWS - V_HEAD)], axis=-1)
    wvt = wv.reshape(KV_LORA, MLA_HEADS * VT_ROWS).T
    flat = lambda w: w.reshape(w.shape[0], -1).astype(BF16)
    return w_in_ext.astype(BF16), flat(wq1), flat(wk), wvt.astype(BF16)


def _mla_layer(x, modt, groups, layer, norm_g3, cache, cos, sin, w_in, q_norm, kv_norm, wq_b, wkv_b, wo):
    t = x.shape[0]
    tm = TOKEN_TILE
    slot = 3 * layer + 1
    w_in_ext, wq1, wk, wvt = _mla_weights(w_in, wq_b, wkv_b)
    hd = MLA_HEADS * HEAD_PAD
    vr = MLA_HEADS * VT_ROWS
    row = lambda c: pl.BlockSpec((tm, c), lambda i: (i, 0))
    q, k, vt, ckv, kr = pl.pallas_call(
        _mla_proj_kernel,
        grid=(t // tm,),
        in_specs=[
            row(D_MODEL), _mod_spec(groups, tm, slot), _const_spec((1, D_MODEL), (slot,)),
            _const_spec(w_in_ext.shape), _const_spec((1, Q_LORA)), _const_spec((1, KV_LORA)),
            _const_spec(wq1.shape), _const_spec(wk.shape), _const_spec(wvt.shape),
            row(HEAD_PAD), row(HEAD_PAD),
        ],
        out_specs=[row(hd), row(hd), pl.BlockSpec((vr, tm), lambda i: (0, i)), row(KV_LORA), row(HEAD_PAD)],
        out_shape=[jax.ShapeDtypeStruct((t, hd), BF16), jax.ShapeDtypeStruct((t, hd), BF16),
                   jax.ShapeDtypeStruct((vr, t), BF16), jax.ShapeDtypeStruct((t, KV_LORA), F32),
                   jax.ShapeDtypeStruct((t, HEAD_PAD), F32)],
        compiler_params=_cparams(("arbitrary",)),
        name="mla_project",
    )(x, modt, norm_g3, w_in_ext, q_norm.reshape(1, -1), kv_norm.reshape(1, -1),
      wq1, wk, wvt, cos, sin)

    prompt, sample = groups
    nb, past, _ = cache.shape
    cflat = cache.reshape(nb * past, -1)
    ckv_c = cflat[:, :KV_LORA]
    kr_c = jnp.pad(cflat[:, KV_LORA:], ((0, 0), (QK_NOPE, HEAD_PAD - QK_HEAD)))
    crow = lambda c: pl.BlockSpec((past, c), lambda i: (i, 0))
    k_c, vt_c = pl.pallas_call(
        _mla_ctx_kernel,
        grid=(nb,),
        in_specs=[crow(KV_LORA), crow(HEAD_PAD), _const_spec(wk.shape), _const_spec(wvt.shape)],
        out_specs=[crow(hd), pl.BlockSpec((vr, past), lambda i: (0, i))],
        out_shape=[jax.ShapeDtypeStruct((nb * past, hd), BF16), jax.ShapeDtypeStruct((vr, nb * past), BF16)],
        compiler_params=_cparams(("arbitrary",)),
        name="mla_ctx_expand",
    )(ckv_c, kr_c, wk, wvt)

    ot_p = _mla_attention(q, [(k, vt, prompt.row0, prompt.length)], prompt, 256, MLA_HEADS)
    ot_s = _mla_attention(q, [(k_c, vt_c, 0, past), (k, vt, sample.row0, sample.length)], sample, 512, 8)
    vd = MLA_HEADS * V_HEAD
    o_specs, n_p = _two_group_specs(groups, tm, vd, axis=1)
    x = pl.pallas_call(
        functools.partial(_mla_out_kernel, n_prompt_tiles=n_p),
        grid=(t // tm,),
        in_specs=o_specs + [_const_spec((vd, D_MODEL)), row(D_MODEL), _mod_spec(groups, tm, slot)],
        out_specs=row(D_MODEL),
        out_shape=jax.ShapeDtypeStruct((t, D_MODEL), F32),
        compiler_params=_cparams(("arbitrary",)),
        name="mla_out",
    )(ot_p, ot_s, wo.astype(BF16), x, modt)
    entry = jnp.concatenate([ckv[:prompt.rows], kr[:prompt.rows, QK_NOPE:QK_HEAD]], axis=-1)
    return x, entry.reshape(prompt.nb, prompt.length, -1)


def _ssm_proj_kernel(x_ref, mod_ref, g_ref, wdt_ref, wt_hbm, z_ref, xbc_ref, dt_ref, wzx_b, stage, sem, *, j):
    rows = stage.shape[1]
    n_slots = stage.shape[0]
    ahead = n_slots - 1
    n_chunks = (D_INNER + CONV_DIM) // rows

    def slab_copy(c):
        return pltpu.make_async_copy(wt_hbm.at[j, pl.ds(c * rows, rows), :], stage.at[c % n_slots],
                                     sem.at[c % n_slots])

    @pl.when(pl.program_id(0) == 0)
    def _load_weights():
        for c in range(ahead):
            slab_copy(c).start()
        for c in range(n_chunks):
            if c + ahead < n_chunks:
                slab_copy(c + ahead).start()
            slab_copy(c).wait()
            for r in range(0, D_MODEL, rows):
                wzx_b[r:r + rows, c * rows:(c + 1) * rows] = stage[c % n_slots, :, r:r + rows].T.astype(BF16)

    m = mod_ref[...]
    hb = _adaln(x_ref[...], g_ref[...], m[0:1], m[1:2]).astype(BF16)
    z_ref[...] = jnp.dot(hb, wzx_b[:, :D_INNER], preferred_element_type=F32).astype(BF16)
    xbc_ref[...] = jnp.dot(hb, wzx_b[:, D_INNER:], preferred_element_type=F32)
    dt_ref[...] = jnp.dot(hb, wdt_ref[...], preferred_element_type=F32)


def _split3(a):
    hi = a.astype(BF16)
    r1 = a - hi.astype(F32)
    mid = r1.astype(BF16)
    lo = (r1 - mid.astype(F32)).astype(BF16)
    return hi, mid, lo


def _ssd_kernel(*refs, zero_init, emit_state, layer, nc):
    it = iter(refs)
    cur_ref, prev_ref, next_ref, dt_ref = next(it), next(it), next(it), next(it)
    cw_ref, cb_ref, dtb_ref, alog_ref, dskip_ref = next(it), next(it), next(it), next(it), next(it)
    h0_ref = None if zero_init else next(it)
    prev_st_ref = next(it) if layer else None
    y_ref = next(it)
    st_ref = next(it) if emit_state else None
    s_f, s_b, sb_loc, y_keep, c_keep, ecb_keep, etb_keep, xpad, act = (next(it) for _ in range(9))

    q = CHUNK
    step = pl.program_id(1)
    npairs = SSM_HEADS // 2
    hpg = SSM_HEADS // SSM_GROUPS
    lane = lax.broadcasted_iota(jnp.int32, (q, LANES), 1)
    left = lane < SSM_HEADDIM

    def pair_cols(v, p):
        return jnp.where(left[:v.shape[0]], v[:, 2 * p:2 * p + 1], v[:, 2 * p + 1:2 * p + 2])

    @pl.when(step == 0)
    def _init():
        if zero_init:
            s_f[...] = jnp.zeros(s_f.shape, F32)
            s_b[...] = jnp.zeros(s_b.shape, F32)
        else:
            s_f[...] = h0_ref[0]
            s_b[...] = h0_ref[1]

    @pl.when(step < nc)
    def _sweep_up():
        c = step
        blk = 4 * SUBLANES
        st = blk // SUBLANES
        has_prev = c > 0
        has_next = c < nc - 1
        for j in range(CONV_DIM // LANES):
            cols = slice(j * LANES, (j + 1) * LANES)
            xpad[j, 0:SUBLANES, :] = jnp.where(has_prev, prev_ref[:, cols], 0.0)
            xpad[j, SUBLANES:SUBLANES + q, :] = cur_ref[:, cols]
            xpad[j, SUBLANES + q:, :] = jnp.where(has_next, next_ref[:, cols], 0.0)
            wk = [jnp.broadcast_to(cw_ref[k:k + 1, cols], (SUBLANES, LANES)) for k in range(CONV_W)]
            bias = jnp.broadcast_to(cb_ref[:, cols], (SUBLANES, LANES))
            for b in range(q // blk):
                r0 = SUBLANES + b * blk - CONV_W // 2
                taps = [xpad[j, pl.ds(r0 + u, SUBLANES, stride=st), :] for u in range(st + CONV_W - 1)]
                for v in range(st):
                    acc = bias
                    for k in range(CONV_W):
                        acc = acc + taps[v + k] * wk[k]
                    act[j, pl.ds(b * blk + v, SUBLANES, stride=st), :] = _silu(acc)
        n_x = D_INNER // LANES
        for g in range(SSM_GROUPS):
            c_keep[c, :, g * D_STATE:(g + 1) * D_STATE] = act[n_x + SSM_GROUPS + g].astype(BF16)

        dtr = dt_ref[...] + dtb_ref[...]
        dtv = jnp.maximum(dtr, 0.0) + jnp.log1p(jnp.exp(-jnp.abs(dtr)))
        a2 = dtv * (-LOG2E * jnp.exp(alog_ref[...]))
        ri = lax.broadcasted_iota(jnp.int32, (q, q), 0)
        ci = lax.broadcasted_iota(jnp.int32, (q, q), 1)
        lower = ci <= ri
        slower = ci < ri
        supper = ci > ri
        tot = jnp.sum(a2, axis=0, keepdims=True)
        dirs = []
        for d, keep in enumerate((lower, ci >= ri)):
            ad = a2[:, d * LANES:(d + 1) * LANES]
            dtd = dtv[:, d * LANES:(d + 1) * LANES]
            td = tot[:, d * LANES:(d + 1) * LANES]
            tri = keep.astype(BF16)
            cum = sum(jnp.dot(tri, part, preferred_element_type=F32) for part in _split3(ad))
            dirs.append(dict(cum=cum, cum_t=cum.T, dt_t=dtd.T, ecum=jnp.exp2(cum), etot=jnp.exp2(td),
                             wout_t=(jnp.exp2(td - cum) * dtd).T))
        fw, bw = dirs
        dt_sum_t = fw["dt_t"] + bw["dt_t"]
        ecb_keep[c] = bw["ecum"]
        etb_keep[c] = jnp.broadcast_to(bw["etot"], (SUBLANES, LANES))
        dsum = dskip_ref[0:1, :] + dskip_ref[1:2, :]

        for g in range(SSM_GROUPS):
            bm = act[n_x + g]
            cm_b = c_keep[c, :, g * D_STATE:(g + 1) * D_STATE]
            bm_t = bm.T
            cb = lax.dot_general(cm_b, bm.astype(BF16), NT, preferred_element_type=F32)
            for pp in range(hpg // 2):
                p = g * (hpg // 2) + pp
                xp = act[p]
                xbd = jnp.concatenate([jnp.where(left, xp, 0.0), jnp.where(left, 0.0, xp)], axis=0).astype(BF16)
                ms, bfs, bbs = [], [], []
                for h in (2 * p, 2 * p + 1):
                    seg = jnp.where(lower, fw["cum"][:, h:h + 1] - fw["cum_t"][h:h + 1, :],
                                    bw["cum"][:, h:h + 1] - bw["cum_t"][h:h + 1, :])
                    wgt = jnp.where(slower, fw["dt_t"][h:h + 1, :],
                                    jnp.where(supper, bw["dt_t"][h:h + 1, :], dt_sum_t[h:h + 1, :]))
                    ms.append((cb * jnp.exp2(seg) * wgt).astype(BF16))
                    bfs.append((bm_t * fw["wout_t"][h:h + 1, :]).astype(BF16))
                    bbs.append((bm_t * bw["wout_t"][h:h + 1, :]).astype(BF16))
                y = jnp.dot(jnp.concatenate(ms, axis=1), xbd, preferred_element_type=F32)
                st = s_f[p]
                y_off = jnp.dot(cm_b, st.astype(BF16), preferred_element_type=F32)
                y_keep[c, :, p * LANES:(p + 1) * LANES] = (y + y_off * pair_cols(fw["ecum"], p)
                                                        + dsum[:, p * LANES:(p + 1) * LANES] * xp)
                s_f[p] = st * pair_cols(fw["etot"], p) + jnp.dot(jnp.concatenate(bfs, axis=1), xbd,
                                                                 preferred_element_type=F32)
                sb_loc[c, p] = jnp.dot(jnp.concatenate(bbs, axis=1), xbd, preferred_element_type=F32)

    @pl.when(step >= nc)
    def _sweep_down():
        c = 2 * nc - 1 - step
        ecb = ecb_keep[c]
        etb = etb_keep[c][0:1]
        for p in range(npairs):
            g = p // (hpg // 2)
            st = s_b[p]
            y_off = jnp.dot(c_keep[c, :, g * D_STATE:(g + 1) * D_STATE], st.astype(BF16),
                            preferred_element_type=F32)
            y_ref[:, p * LANES:(p + 1) * LANES] = (y_keep[c, :, p * LANES:(p + 1) * LANES]
                                                    + y_off * pair_cols(ecb, p))
            s_b[p] = st * pair_cols(etb, p) + sb_loc[c, p]

        if emit_state:
            @pl.when(step == 2 * nc - 1)
            def _emit():
                for l in range(layer):
                    st_ref[l] = prev_st_ref[l]
                for p in range(npairs):
                    st_ref[layer, 0, p] = s_f[p].T
                    st_ref[layer, 1, p] = s_b[p].T


def _ssd_scan(xbc, dt, grp, conv_w, conv_b, dt_bias, a_log, d_skip, h0, state_out):
    t = xbc.shape[0]
    q = CHUNK
    nc = grp.length // q
    zero_init = h0 is None
    emit_state = state_out is not None
    boff = grp.row0 // q
    npairs = SSM_HEADS // 2
    qs = q // SUBLANES

    chunk = lambda b, s: boff + b * nc + jnp.minimum(s, nc - 1)
    in_specs = [
        pl.BlockSpec((q, CONV_DIM), lambda b, s: (chunk(b, s), 0)),
        pl.BlockSpec((SUBLANES, CONV_DIM), lambda b, s: (jnp.maximum(chunk(b, s) * qs - 1, 0), 0)),
        pl.BlockSpec((SUBLANES, CONV_DIM), lambda b, s: (jnp.minimum((chunk(b, s) + 1) * qs, t // SUBLANES - 1), 0)),
        pl.BlockSpec((q, 2 * LANES), lambda b, s: (chunk(b, s), 0)),
        _const_spec((SUBLANES, CONV_DIM)), _const_spec((1, CONV_DIM)),
        _const_spec((1, 2 * LANES)), _const_spec((1, 2 * LANES)), _const_spec((2, D_INNER)),
    ]
    args = [xbc, xbc, xbc, dt, conv_w, conv_b, dt_bias, a_log, d_skip]
    if not zero_init:
        in_specs.append(pl.BlockSpec((None, 2, npairs, D_STATE, LANES), lambda b, s: (b, 0, 0, 0, 0)))
        args.append(h0)
    state_block = lambda n: pl.BlockSpec((None, n, 2, npairs, LANES, D_STATE), lambda b, s: (b, 0, 0, 0, 0, 0))
    layer = state_out[1] if emit_state else 0
    if layer:
        in_specs.append(state_block(layer))
        args.append(state_out[0])
    out_specs = [
        pl.BlockSpec((q, D_INNER), lambda b, s: (b * nc + jnp.where(s < nc, nc - 1, 2 * nc - 1 - s), 0)),
    ]
    out_shape = [jax.ShapeDtypeStruct((grp.rows, D_INNER), F32)]
    if emit_state:
        out_specs.append(state_block(layer + 1))
        out_shape.append(jax.ShapeDtypeStruct((grp.nb, layer + 1, 2, npairs, LANES, D_STATE), F32))
    return pl.pallas_call(
        functools.partial(_ssd_kernel, zero_init=zero_init, emit_state=emit_state, layer=layer, nc=nc),
        grid=(grp.nb, 2 * nc),
        in_specs=in_specs,
        out_specs=out_specs,
        out_shape=out_shape,
        scratch_shapes=[pltpu.VMEM((npairs, D_STATE, LANES), F32),
                        pltpu.VMEM((npairs, D_STATE, LANES), F32),
                        pltpu.VMEM((nc, npairs, D_STATE, LANES), F32),
                        pltpu.VMEM((nc, q, D_INNER), F32),
                        pltpu.VMEM((nc, q, SSM_GROUPS * D_STATE), BF16),
                        pltpu.VMEM((nc, q, LANES), F32),
                        pltpu.VMEM((nc, SUBLANES, LANES), F32),
                        pltpu.VMEM((CONV_DIM // LANES, q + 2 * SUBLANES, LANES), F32),
                        pltpu.VMEM((CONV_DIM // LANES, q, LANES), F32)],
        compiler_params=_cparams(("arbitrary", "arbitrary")),
        name="ssd_scan",
    )(*args)


def _ssm_out_kernel(yp_ref, ys_ref, z_ref, ng_ref, w_ref, x_ref, mod_ref, o_ref, *, n_prompt_tiles):
    y = jnp.where(pl.program_id(0) < n_prompt_tiles, yp_ref[...], ys_ref[...])
    y = y * _silu(z_ref[...].astype(F32))
    yn = _rmsnorm(y, ng_ref[...]).astype(BF16)
    o_ref[...] = x_ref[...] + mod_ref[...][2:3] * jnp.dot(yn, w_ref[...], preferred_element_type=F32)


def _pair_state(h):
    b = h.shape[0]
    return h.reshape(b, 2, SSM_HEADS // 2, 2, SSM_HEADDIM, D_STATE).transpose(0, 1, 2, 5, 3, 4).reshape(
        b, 2, SSM_HEADS // 2, D_STATE, 2 * SSM_HEADDIM)


def _ssm_layer(x, modt, groups, layer, norm_g3, h0_sample, state_out, weights, conv_w, conv_b,
               dt_bias, a_log, d_skip, norm_g):
    j, w_in, w_dt, w_out = weights
    t = x.shape[0]
    tm = TOKEN_TILE
    slot = 3 * layer + 1
    row = lambda c: pl.BlockSpec((tm, c), lambda i: (i, 0))
    z, xbc, dt = pl.pallas_call(
        functools.partial(_ssm_proj_kernel, j=j),
        grid=(t // tm,),
        in_specs=[row(D_MODEL), _mod_spec(groups, tm, slot), _const_spec((1, D_MODEL), (slot,)),
                  _const_spec(w_dt.shape[1:], (j,)), pl.BlockSpec(memory_space=pl.ANY)],
        out_specs=[row(D_INNER), row(CONV_DIM), row(2 * LANES)],
        out_shape=[jax.ShapeDtypeStruct((t, D_INNER), BF16), jax.ShapeDtypeStruct((t, CONV_DIM), F32),
                   jax.ShapeDtypeStruct((t, 2 * LANES), F32)],
        scratch_shapes=[pltpu.VMEM((D_MODEL, D_INNER + CONV_DIM), BF16),
                        pltpu.VMEM((SSM_STAGE_SLOTS, SSM_STAGE_ROWS, D_MODEL), F32),
                        pltpu.SemaphoreType.DMA((SSM_STAGE_SLOTS,))],
        compiler_params=_cparams(("arbitrary",)),
        name="ssm_project",
    )(x, modt, norm_g3, w_dt, w_in)

    lane_pad = lambda v: jnp.pad(v, ((0, 0), (0, LANES - v.shape[-1]))).reshape(1, 2 * LANES)
    cw = jnp.pad(conv_w, ((0, SUBLANES - CONV_W), (0, 0)))
    cb = conv_b.reshape(1, CONV_DIM)
    dsk = jnp.repeat(d_skip, SSM_HEADDIM, axis=-1)
    prompt, sample = groups
    scan = functools.partial(_ssd_scan, xbc, dt)
    yp, new_state = scan(prompt, cw, cb, lane_pad(dt_bias), lane_pad(a_log), dsk, None, state_out)
    ys, = scan(sample, cw, cb, lane_pad(dt_bias), lane_pad(a_log), dsk, _pair_state(h0_sample), None)

    to = TOKEN_TILE
    row = lambda c: pl.BlockSpec((to, c), lambda i: (i, 0))
    y_specs, n_p = _two_group_specs(groups, to, D_INNER)
    x = pl.pallas_call(
        functools.partial(_ssm_out_kernel, n_prompt_tiles=n_p),
        grid=(t // to,),
        in_specs=y_specs + [row(D_INNER), _const_spec((1, D_INNER)),
                  _const_spec((D_INNER, D_MODEL), (j,)), row(D_MODEL), _mod_spec(groups, to, slot)],
        out_specs=row(D_MODEL),
        out_shape=jax.ShapeDtypeStruct((t, D_MODEL), F32),
        compiler_params=_cparams(("arbitrary",)),
        name="ssm_out",
    )(yp, ys, z, norm_g.reshape(1, D_INNER), w_out, x, modt)
    return x, new_state


def kernel(x_prompt, x_sample, cache_mla, state_ssm, c, c_ctx, mod_w, mod_b, norm_g, ffn_w_in, ffn_w_out,
           mla_w_in, mla_q_norm, mla_kv_norm, mla_wq_b, mla_wkv_b, mla_wo, ssm_w_in, ssm_conv_w, ssm_conv_b,
           ssm_dt_bias, ssm_a_log, ssm_d, ssm_norm_g, ssm_w_out, final_norm_g):
    nbp, lp, d = x_prompt.shape
    nbs, ls, _ = x_sample.shape
    prompt = _Group(0, nbp, lp)
    sample = _Group(nbp * lp, nbs, ls)
    groups = (prompt, sample)
    x = (x_prompt.reshape(-1, d), x_sample.reshape(-1, d))

    ncond = 1 + nbs
    sc = jnp.concatenate([c_ctx[None, :], c, jnp.zeros((SUBLANES - ncond, d), F32)], axis=0)
    mod = _modulation(sc, mod_w, mod_b)
    modt = mod[:, :ncond].reshape(DEPTH, ncond, 3, 3, d).transpose(1, 0, 2, 3, 4).reshape(ncond, DEPTH * 3, 3, d)

    norm_g3 = norm_g.reshape(DEPTH * 3, 1, d)
    ffn_in = ffn_w_in
    ffn_out = ffn_w_out
    n_ssm = ssm_w_in.shape[0]
    ssm_w_in_t = jnp.swapaxes(ssm_w_in, 1, 2)
    wdt = ssm_w_in[:, :, D_INNER + CONV_DIM:].reshape(n_ssm, d, 2, SSM_HEADS)
    ssm_wdt = jnp.pad(wdt, ((0, 0), (0, 0), (0, 0), (0, LANES - SSM_HEADS))).reshape(n_ssm, d, 2 * LANES).astype(BF16)
    ssm_wo = ssm_w_out.astype(BF16)

    cos, sin = _rope_tables(groups)
    new_mla = []
    new_state = None
    tiles_p = prompt.rows // FFN_TILE
    tiles_s = sample.rows // FFN_TILE
    for i in range(DEPTH):
        j = i // 2
        x = _ffn_half(x, modt, groups, i, 0, norm_g3, ffn_in, ffn_out, final_norm_g)
        if i % 2 == 0:
            x, entry = _mla_layer(x, modt, groups, i, norm_g3, cache_mla[:, j], cos, sin,
                                  mla_w_in[j], mla_q_norm[j], mla_kv_norm[j], mla_wq_b[j], mla_wkv_b[j], mla_wo[j])
            new_mla.append(entry)
        else:
            x, new_state = _ssm_layer(x, modt, groups, i, norm_g3, state_ssm[:, j], (new_state, j),
                                      (j, ssm_w_in_t, ssm_wdt, ssm_wo), ssm_conv_w[j], ssm_conv_b[j],
                                      ssm_dt_bias[j], ssm_a_log[j], ssm_d[j], ssm_norm_g[j])
        if i < DEPTH - 1:
            x = _ffn_half(x, modt, groups, i, 1, norm_g3, ffn_in, ffn_out, final_norm_g)
    last = functools.partial(_ffn_half, x, modt, groups, DEPTH - 1, 1, norm_g3, ffn_in, ffn_out, final_norm_g, True)
    y_prompt = last((0, tiles_p)).reshape(nbp, lp, d)
    y_sample = last((tiles_p, tiles_s)).reshape(nbs, ls, d)
    new_state = new_state.reshape(nbp, n_ssm, 2, SSM_HEADS, SSM_HEADDIM, D_STATE)
    return y_prompt, y_sample, jnp.stack(new_mla, axis=1), new_state
```
